```python
import math
import jax, jax.numpy as jnp
from jax import lax
import numpy as np

D_MODEL = 2048
BATCH = 4
SEQ = 2048
DEPTH = 1
DEC_BATCH = 32
DEC_SEQ = 4
PAST_LEN = 16384
PAGE_SIZE = 128

A_HEADS = 8
A_HEAD_DIM = 128
A_WIDTH = A_HEADS * A_HEAD_DIM
MOBA_BLOCK = 256
MOBA_TOPK = 3
MOBA_Q_CHUNK = 32
R_HEADS = 8
R_QK_DIM = 64
R_V_DIM = 128
R_QK_WIDTH = R_HEADS * R_QK_DIM
R_WIDTH = R_HEADS * R_V_DIM
R_CHUNK = 128
MIX_WIDTH = A_WIDTH + R_WIDTH
IN_SPLITS = (A_WIDTH, 2 * A_WIDTH, 3 * A_WIDTH, 3 * A_WIDTH + R_QK_WIDTH,
             3 * A_WIDTH + 2 * R_QK_WIDTH, 3 * A_WIDTH + 2 * R_QK_WIDTH + R_WIDTH)
IN_COLS = 3 * A_WIDTH + 2 * R_QK_WIDTH + 2 * R_WIDTH
P_HEADS = 8
P_NKEYS = 128
P_EXPERTS = P_NKEYS * P_NKEYS
P_KEY_DIM = 256
P_HALF = P_KEY_DIM // 2
P_TOPK = 16
P_TOKEN_BLOCK = 128

EPS = 1e-6
NEG = -1e30
F32 = jnp.float32

kernel_name = 'hymba_moba_retnet_peer_adaln_step'


def _rms_norm(x, w):
    xf = x.astype(F32)
    y = xf * lax.rsqrt(jnp.mean(xf * xf, axis=-1, keepdims=True) + EPS)
    return (y * w.astype(F32)).astype(x.dtype)


def _modulate(h, shift, scale):
    return h * (1 + scale[:, None, :]) + shift[:, None, :]


def _alibi_slopes(n):
    return jnp.exp2(-8.0 * jnp.arange(1, n + 1, dtype=F32) / n)


def _retention_log_decay(n):
    return jnp.log1p(-jnp.exp2(-5.0 - jnp.arange(n, dtype=F32)))


def _key_sequence(new, pool, page_table):
    B, T, H, dh = new.shape
    past = 0 if pool is None else page_table.shape[1] * pool.shape[1]
    pad = (-(past + T)) % MOBA_BLOCK
    parts = [] if pool is None else [pool[page_table].reshape(B, past, H, dh).astype(new.dtype)]
    parts = parts + [new, jnp.zeros((B, pad, H, dh), new.dtype)]
    return jnp.concatenate(parts, axis=1), past


def _moba_attention(q, k_seq, v_seq, q_pos):
    B, Tq, H, dh = q.shape
    n_blk = k_seq.shape[1] // MOBA_BLOCK
    kb = k_seq.reshape(B, n_blk, MOBA_BLOCK, H, dh)
    vb = v_seq.reshape(B, n_blk, MOBA_BLOCK, H, dh)
    k_mean = jnp.mean(kb, axis=2, dtype=F32)
    n_sel = min(MOBA_TOPK, n_blk)
    slopes = _alibi_slopes(H)[None, :, None, None, None]
    scale = dh ** -0.5
    qc = math.gcd(Tq, MOBA_Q_CHUNK)
    n_c = Tq // qc
    q_chunks = q.reshape(B, n_c, qc, H, dh).transpose(1, 0, 3, 2, 4)
    pos_chunks = q_pos.reshape(n_c, qc)
    b_idx = jnp.arange(B)[:, None, None, None]
    h_idx = jnp.arange(H)[None, :, None, None]
    blk_ids = jnp.arange(n_blk)
    offs = jnp.arange(MOBA_BLOCK, dtype=jnp.int32)

    def one_chunk(args):
        qq, pp = args
        qf = qq.astype(F32)
        own = pp // MOBA_BLOCK
        gate = jnp.einsum('bhqd,bnhd->bhqn', qf, k_mean)
        gate = jnp.where(blk_ids[None, :] < own[:, None], gate, NEG)
        _, top = lax.top_k(gate, n_sel)
        own_b = jnp.broadcast_to(own[None, None, :, None], (B, H, qc, 1)).astype(top.dtype)
        blocks = jnp.concatenate([top, own_b], axis=-1)
        blk_ok = jnp.concatenate([top < own_b, jnp.ones_like(own_b, dtype=bool)], axis=-1)
        kg = kb[b_idx, blocks, :, h_idx, :].astype(F32)
        vg = vb[b_idx, blocks, :, h_idx, :].astype(F32)
        dist = pp[None, None, :, None, None] - (blocks[..., None] * MOBA_BLOCK + offs)
        valid = blk_ok[..., None] & (dist >= 0)
        logits = jnp.einsum('bhqd,bhqjsd->bhqjs', qf, kg) * scale - slopes * dist.astype(F32)
        logits = jnp.where(valid, logits, NEG).reshape(B, H, qc, -1)
        probs = jax.nn.softmax(logits, axis=-1)
        out = jnp.einsum('bhqs,bhqsd->bhqd', probs, vg.reshape(B, H, qc, -1, dh))
        return out.astype(q.dtype)

    out = lax.map(one_chunk, (q_chunks, pos_chunks))
    return out.transpose(1, 0, 3, 2, 4).reshape(B, Tq, H * dh)


def _retention(q, k, v, s0, log_gamma):
    B, T, H, dk = q.shape
    dv = v.shape[-1]
    c = math.gcd(T, R_CHUNK)
    n = T // c
    idx = jnp.arange(c, dtype=F32)
    diff = idx[:, None] - idx[None, :]
    causal = diff >= 0
    decay_in = jnp.where(causal[None], jnp.exp(jnp.where(causal, diff, 0.0)[None] * log_gamma[:, None, None]), 0.0)
    q_dec = jnp.exp((idx + 1)[None, :] * log_gamma[:, None])[None, :, :, None]
    k_dec = jnp.exp((c - 1 - idx)[None, :] * log_gamma[:, None])[None, :, :, None]
    chunk_dec = jnp.exp(c * log_gamma)[None, :, None, None]

    def to_chunks(a):
        return a.reshape(B, n, c, H, a.shape[-1]).transpose(1, 0, 3, 2, 4).astype(F32)

    def step(s, inp):
        qc, kc, vc = inp
        inner = jnp.einsum('bhid,bhjd->bhij', qc, kc) * decay_in[None]
        o = jnp.einsum('bhij,bhjv->bhiv', inner, vc) + jnp.einsum('bhid,bhdv->bhiv', qc, s) * q_dec
        s = s * chunk_dec + jnp.einsum('bhjd,bhjv->bhdv', kc * k_dec, vc)
        return s, o

    s, o = lax.scan(step, s0.astype(F32), (to_chunks(q), to_chunks(k), to_chunks(v)))
    o = o.transpose(1, 0, 3, 2, 4).reshape(B, T, H, dv).astype(q.dtype)
    return o, s


def _peer(h, w_query, sub_keys, expert_down, expert_up):
    B, T, D = h.shape
    n_tok = B * T
    x = h.reshape(n_tok, D)
    q = (x @ w_query).reshape(n_tok, P_HEADS, 2, P_HALF).astype(F32)
    sc = jnp.einsum('nhpd,hpkd->nhpk', q, sub_keys.astype(F32))
    s1, i1 = lax.top_k(sc[:, :, 0], P_TOPK)
    s2, i2 = lax.top_k(sc[:, :, 1], P_TOPK)
    cand = (s1[..., :, None] + s2[..., None, :]).reshape(n_tok, P_HEADS, P_TOPK * P_TOPK)
    cand_idx = (i1[..., :, None] * P_NKEYS + i2[..., None, :]).reshape(n_tok, P_HEADS, P_TOPK * P_TOPK)
    best, pos = lax.top_k(cand, P_TOPK)
    experts = jnp.take_along_axis(cand_idx, pos, axis=-1)
    gates = jax.nn.softmax(best, axis=-1)
    tb = math.gcd(n_tok, P_TOKEN_BLOCK)
    nb = n_tok // tb

    def block(args):
        xb, eb, gb = args
        act = jax.nn.gelu(jnp.einsum('td,thkd->thk', xb.astype(F32), expert_down[eb].astype(F32)), approximate=False)
        return jnp.einsum('thk,thkd->td', act * gb, expert_up[eb].astype(F32)).astype(xb.dtype)

    y = lax.map(block, (x.reshape(nb, tb, D), experts.reshape(nb, tb, P_HEADS, P_TOPK),
                        gates.reshape(nb, tb, P_HEADS, P_TOPK)))
    return y.reshape(B, T, D)


def _layer(x, c, pool_k, pool_v, page_table, ret_s0, p):
    B, T, _ = x.shape
    mod = (jax.nn.silu(c.astype(F32)) @ p['w_ada'].astype(F32) + p['b_ada'].astype(F32)).astype(x.dtype)
    sh1, sc1, g1, sh2, sc2, g2 = jnp.split(mod, 6, axis=-1)
    h = _modulate(_rms_norm(x, p['norm1_w']), sh1, sc1)
    qa, ka, va, qr, kr, vr, gr = jnp.split(h @ p['w_in'], IN_SPLITS, axis=-1)
    qa = _rms_norm(qa.reshape(B, T, A_HEADS, A_HEAD_DIM), p['q_norm_w'])
    ka = _rms_norm(ka.reshape(B, T, A_HEADS, A_HEAD_DIM), p['k_norm_w'])
    va = va.reshape(B, T, A_HEADS, A_HEAD_DIM)
    k_seq, past = _key_sequence(ka, pool_k, page_table)
    v_seq, _ = _key_sequence(va, pool_v, page_table)
    q_pos = past + jnp.arange(T, dtype=jnp.int32)
    out_a = _moba_attention(qa, k_seq, v_seq, q_pos)
    qr = qr.reshape(B, T, R_HEADS, R_QK_DIM)
    kr = kr.reshape(B, T, R_HEADS, R_QK_DIM) * (R_QK_DIM ** -0.5)
    vr = vr.reshape(B, T, R_HEADS, R_V_DIM)
    out_r, s_new = _retention(qr, kr, vr, ret_s0, _retention_log_decay(R_HEADS))
    out_r = _rms_norm(out_r, p['ret_norm_w']).reshape(B, T, R_WIDTH) * jax.nn.silu(gr)
    mixed = jnp.concatenate([out_a, out_r], axis=-1) @ p['w_out']
    x = x + g1[:, None, :] * mixed
    h2 = _modulate(_rms_norm(x, p['norm2_w']), sh2, sc2)
    x = x + g2[:, None, :] * _peer(h2, p['peer_w_query'], p['peer_sub_keys'], p['peer_down'], p['peer_up'])
    return x, ka, va, s_new


def setup_inputs(seed: int = 0) -> dict:
    key = jax.random.key(seed)
    ks = jax.random.split(key, 24)
    nrm = jax.random.normal
    n_pages = PAST_LEN // PAGE_SIZE
    n_used = DEC_BATCH * n_pages
    n_pool = n_used + n_used // 4 + 1
    page_table = jax.random.permutation(ks[0], n_pool)[:n_used].reshape(DEC_BATCH, n_pages).astype(jnp.int32)
    L = DEPTH
    return {
        'x_prompt': nrm(ks[1], (BATCH, SEQ, D_MODEL), F32),
        'x_sample': nrm(ks[2], (DEC_BATCH, DEC_SEQ, D_MODEL), F32),
        'cache_k': nrm(ks[3], (L, n_pool, PAGE_SIZE, A_HEADS, A_HEAD_DIM), F32),
        'cache_v': nrm(ks[4], (L, n_pool, PAGE_SIZE, A_HEADS, A_HEAD_DIM), F32),
        'state_ret': nrm(ks[5], (L, DEC_BATCH, R_HEADS, R_QK_DIM, R_V_DIM), F32) * 0.5,
        'page_table': page_table,
        'c_prompt': nrm(ks[6], (BATCH, D_MODEL), F32),
        'c_sample': nrm(ks[7], (DEC_BATCH, D_MODEL), F32),
        'w_ada': nrm(ks[8], (L, D_MODEL, 6 * D_MODEL), F32) * 0.01,
        'b_ada': nrm(ks[9], (L, 6 * D_MODEL), F32) * 0.01,
        'norm1_w': 1.0 + 0.01 * nrm(ks[10], (L, D_MODEL), F32),
        'w_in': nrm(ks[11], (L, D_MODEL, IN_COLS), F32) * D_MODEL ** -0.5,
        'q_norm_w': 1.0 + 0.01 * nrm(ks[12], (L, A_HEAD_DIM), F32),
        'k_norm_w': 1.0 + 0.01 * nrm(ks[13], (L, A_HEAD_DIM), F32),
        'ret_norm_w': 1.0 + 0.01 * nrm(ks[14], (L, R_V_DIM), F32),
        'w_out': nrm(ks[15], (L, MIX_WIDTH, D_MODEL), F32) * MIX_WIDTH ** -0.5,
        'norm2_w': 1.0 + 0.01 * nrm(ks[16], (L, D_MODEL), F32),
        'peer_w_query': nrm(ks[17], (L, D_MODEL, P_HEADS * P_KEY_DIM), F32) * D_MODEL ** -0.5,
        'peer_sub_keys': nrm(ks[18], (L, P_HEADS, 2, P_NKEYS, P_HALF), F32) * P_HALF ** -0.5,
        'peer_down': nrm(ks[19], (L, P_EXPERTS, D_MODEL), F32) * D_MODEL ** -0.5,
        'peer_up': nrm(ks[20], (L, P_EXPERTS, D_MODEL), F32) * P_HEADS ** -0.5,
    }


def reference(x_prompt, x_sample, cache_k, cache_v, state_ret, page_table, c_prompt, c_sample,
              w_ada, b_ada, norm1_w, w_in, q_norm_w, k_norm_w, ret_norm_w, w_out, norm2_w,
              peer_w_query, peer_sub_keys, peer_down, peer_up):
    hp, hs = x_prompt, x_sample
    kp, vp, sp, ksm, vsm, ssm = [], [], [], [], [], []
    s0_prompt = jnp.zeros((x_prompt.shape[0], R_HEADS, R_QK_DIM, R_V_DIM), F32)
    for l in range(DEPTH):
        p = dict(w_ada=w_ada[l], b_ada=b_ada[l], norm1_w=norm1_w[l], w_in=w_in[l], q_norm_w=q_norm_w[l],
                 k_norm_w=k_norm_w[l], ret_norm_w=ret_norm_w[l], w_out=w_out[l], norm2_w=norm2_w[l],
                 peer_w_query=peer_w_query[l], peer_sub_keys=peer_sub_keys[l], peer_down=peer_down[l],
                 peer_up=peer_up[l])
        hp, k1, v1, s1 = _layer(hp, c_prompt, None, None, None, s0_prompt, p)
        hs, k2, v2, s2 = _layer(hs, c_sample, cache_k[l], cache_v[l], page_table, state_ret[l], p)
        kp.append(k1); vp.append(v1); sp.append(s1)
        ksm.append(k2); vsm.append(v2); ssm.append(s2)
    return (hp, hs, jnp.stack(kp), jnp.stack(vp), jnp.stack(sp), jnp.stack(ksm), jnp.stack(vsm), jnp.stack(ssm))
```

```python
import functools
import math

import jax
import jax.numpy as jnp
from jax import lax
from jax.experimental import pallas as pl
from jax.experimental.pallas import tpu as pltpu

F32 = jnp.float32
BF16 = jnp.bfloat16
I32 = jnp.int32

EPS = 1e-6
NEG = -1e30

A_HEADS = 8
A_HEAD_DIM = 128
A_WIDTH = A_HEADS * A_HEAD_DIM
MOBA_BLOCK = 256
MOBA_TOPK = 3
R_HEADS = 8
R_QK_DIM = 64
R_V_DIM = 128
R_QK_WIDTH = R_HEADS * R_QK_DIM
R_WIDTH = R_HEADS * R_V_DIM
R_CHUNK = 128
P_HEADS = 8
P_NKEYS = 128
P_TOPK = 16
P_SLOTS = P_HEADS * P_TOPK

LANES = 128
VMEM_LIMIT_BYTES = 48 * 1024 * 1024

NT_DIMS = (((1,), (1,)), ((), ()))


def _params(*semantics):
    return pltpu.CompilerParams(dimension_semantics=semantics, vmem_limit_bytes=VMEM_LIMIT_BYTES)


def _silu(x):
    return x * jax.nn.sigmoid(x)


def _ada_kernel(c_ref, w_ref, b_ref, o_ref):
    s = _silu(c_ref[...]).astype(BF16)
    o_ref[...] = jnp.dot(s, w_ref[...].astype(BF16), preferred_element_type=F32) + b_ref[...]


def _ada(c, w, b):
    rows, d = c.shape
    n = w.shape[1]
    tn = 1024
    return pl.pallas_call(
        _ada_kernel,
        grid=(n // tn,),
        in_specs=[pl.BlockSpec((rows, d), lambda j: (0, 0)),
                  pl.BlockSpec((d, tn), lambda j: (0, j)),
                  pl.BlockSpec((1, tn), lambda j: (0, j))],
        out_specs=pl.BlockSpec((rows, tn), lambda j: (0, j)),
        out_shape=jax.ShapeDtypeStruct((rows, n), F32),
        compiler_params=_params("parallel"),
        name="ada",
    )(c, w, b)


def _inproj_kernel(x_ref, nw_ref, sh_ref, sc_ref, w_ref, aux_ref, o_ref, *, mode):
    x = x_ref[...]
    var = jnp.mean(x * x, axis=-1, keepdims=True)
    h = x * lax.rsqrt(var + EPS) * nw_ref[...]
    h = h * (1 + sc_ref[...]) + sh_ref[...]
    y = jnp.dot(h.astype(BF16), w_ref[...], preferred_element_type=F32)
    if mode == "headnorm":
        for hd in range(y.shape[1] // A_HEAD_DIM):
            yh = y[:, hd * A_HEAD_DIM:(hd + 1) * A_HEAD_DIM]
            v = jnp.mean(yh * yh, axis=-1, keepdims=True)
            o_ref[:, hd * A_HEAD_DIM:(hd + 1) * A_HEAD_DIM] = yh * lax.rsqrt(v + EPS) * aux_ref[...]
    elif mode == "scale":
        o_ref[...] = y * aux_ref[...]
    else:
        o_ref[...] = y


def _inproj(x, norm_w, shift, scale, w, col0, ncols, aux, mode, tm, tiles_per_mod):
    m, d = x.shape
    tn = 1024
    assert ncols % tn == 0 and col0 % tn == 0 and m % tm == 0
    mod_rows = shift.shape[1]
    mod_spec = pl.BlockSpec((None, mod_rows, d), lambda i, j: (i // tiles_per_mod, 0, 0))
    if mode == "headnorm":
        aux_spec = pl.BlockSpec((1, A_HEAD_DIM), lambda i, j: (0, 0))
    else:
        aux_spec = pl.BlockSpec((1, tn), lambda i, j: (0, j))
    return pl.pallas_call(
        functools.partial(_inproj_kernel, mode=mode),
        grid=(m // tm, ncols // tn),
        in_specs=[pl.BlockSpec((tm, d), lambda i, j: (i, 0)),
                  pl.BlockSpec((1, d), lambda i, j: (0, 0)),
                  mod_spec, mod_spec,
                  pl.BlockSpec((d, tn), lambda i, j: (0, col0 // tn + j)),
                  aux_spec],
        out_specs=pl.BlockSpec((tm, tn), lambda i, j: (i, j)),
        out_shape=jax.ShapeDtypeStruct((m, ncols), F32),
        compiler_params=_params("parallel", "arbitrary"),
        name="inproj_" + mode,
    )(x, norm_w, shift, scale, w, aux)


def _moba_prompt_kernel(slope_ref, q_ref, k_ref, v_ref, o_ref, *, seq):
    qi = pl.program_id(2)
    blk = MOBA_BLOCK
    nb = seq // blk
    q = q_ref[...]
    k = k_ref[...]
    kmean = jnp.mean(k.reshape(nb, blk, A_HEAD_DIM), axis=1)
    gate = lax.dot_general(q, kmean, NT_DIMS, precision=lax.Precision.HIGHEST,
                           preferred_element_type=F32)
    col = lax.broadcasted_iota(I32, (blk, nb), 1)
    past = col < qi
    gate = jnp.where(past, gate, NEG)
    cnt = jnp.zeros((blk, nb), F32)
    for m in range(nb):
        gm = gate[:, m:m + 1]
        ahead = jnp.where(gm > gate, 1.0, jnp.where((gm == gate) & (m < col), 1.0, 0.0))
        cnt = cnt + ahead
    sel = jnp.where((past & (cnt < MOBA_TOPK)) | (col == qi), 1.0, 0.0)

    scale = A_HEAD_DIM ** -0.5
    slope = slope_ref[...][:, :1]
    s = lax.dot_general(q.astype(BF16), k.astype(BF16), NT_DIMS, preferred_element_type=F32)
    rc = (lax.broadcasted_iota(I32, (blk, blk), 0) - lax.broadcasted_iota(I32, (blk, blk), 1))
    pieces = []
    for n in range(nb):
        dist = rc + (qi - n) * blk
        ok = (sel[:, n:n + 1] > 0.5) & (dist >= 0)
        logit = s[:, n * blk:(n + 1) * blk] * scale - slope * dist.astype(F32)
        pieces.append(jnp.where(ok, logit, NEG))
    logits = jnp.concatenate(pieces, axis=1)
    mx = jnp.max(logits, axis=1, keepdims=True)
    p = jnp.exp(logits - mx)
    denom = jnp.sum(p, axis=1, keepdims=True)
    out = jnp.dot(p.astype(BF16), v_ref[...].astype(BF16), preferred_element_type=F32)
    o_ref[...] = (out / denom).astype(o_ref.dtype)


def _moba_prompt(qa, ka, va, slopes, batch, seq):
    blk = MOBA_BLOCK
    nq = seq // blk
    return pl.pallas_call(
        functools.partial(_moba_prompt_kernel, seq=seq),
        grid=(batch, A_HEADS, nq),
        in_specs=[pl.BlockSpec((None, 1, LANES), lambda b, h, i: (h, 0, 0)),
                  pl.BlockSpec((blk, A_HEAD_DIM), lambda b, h, i: (b * nq + i, h)),
                  pl.BlockSpec((seq, A_HEAD_DIM), lambda b, h, i: (b, h)),
                  pl.BlockSpec((seq, A_HEAD_DIM), lambda b, h, i: (b, h))],
        out_specs=pl.BlockSpec((blk, A_HEAD_DIM), lambda b, h, i: (b * nq + i, h)),
        out_shape=jax.ShapeDtypeStruct((batch * seq, A_WIDTH), BF16),
        compiler_params=_params("parallel", "parallel", "arbitrary"),
        name="moba_prompt",
    )(slopes, qa, ka, va)


PAGES_PER_STEP = 8


def _page_mean_kernel(pt_ref, *refs):
    del pt_ref
    o_ref = refs[-1]
    pages_per_block = PAGES_PER_STEP // o_ref.shape[0]
    for j in range(o_ref.shape[0]):
        acc = None
        rows = 0
        for u in range(pages_per_block):
            page = refs[j * pages_per_block + u]
            part = jnp.sum(page[...], axis=0)
            rows += page.shape[0]
            acc = part if acc is None else acc + part
        o_ref[j] = acc * (1.0 / rows)


def _page_means(pool, page_table, page_size):
    bsz, n_pages = page_table.shape
    pages_per_block = MOBA_BLOCK // page_size
    blocks_per_step = PAGES_PER_STEP // pages_per_block
    n_blk = n_pages // pages_per_block
    assert n_pages % PAGES_PER_STEP == 0

    def page_spec(u):
        return pl.BlockSpec((None, page_size, A_HEADS, A_HEAD_DIM),
                            lambda b, g, pt: (pt[b, g * PAGES_PER_STEP + u], 0, 0, 0))

    return pl.pallas_call(
        _page_mean_kernel,
        grid_spec=pltpu.PrefetchScalarGridSpec(
            num_scalar_prefetch=1,
            grid=(bsz, n_pages // PAGES_PER_STEP),
            in_specs=[page_spec(u) for u in range(PAGES_PER_STEP)],
            out_specs=pl.BlockSpec((None, blocks_per_step, A_HEADS, A_HEAD_DIM),
                                   lambda b, g, pt: (b, g, 0, 0))),
        out_shape=jax.ShapeDtypeStruct((bsz, n_blk, A_HEADS, A_HEAD_DIM), F32),
        compiler_params=_params("parallel", "arbitrary"),
        name="page_means",
    )(page_table, *([pool] * PAGES_PER_STEP))


def _block_choice_kernel(q_ref, km_ref, o_ref):
    rows = q_ref.shape[0]
    t_new = rows // A_HEADS
    n_blk = km_ref.shape[0]
    q = q_ref[...]
    r_head = lax.broadcasted_iota(I32, q.shape, 0) // t_new
    c_head = lax.broadcasted_iota(I32, q.shape, 1) // A_HEAD_DIM
    qbd = jnp.where(r_head == c_head, q, 0.0)
    gate = lax.dot_general(qbd, km_ref[...], NT_DIMS, precision=lax.Precision.HIGHEST,
                           preferred_element_type=F32)
    lane = lax.broadcasted_iota(I32, gate.shape, 1).astype(F32)
    out_lane = lax.broadcasted_iota(I32, (rows, LANES), 1)
    res = jnp.zeros((rows, LANES), F32)
    for it in range(MOBA_TOPK):
        mx = jnp.max(gate, axis=1, keepdims=True)
        idx = jnp.min(jnp.where(gate == mx, lane, float(n_blk)), axis=1, keepdims=True)
        res = jnp.where(out_lane == it, idx, res)
        gate = jnp.where(lane == idx, -jnp.inf, gate)
    o_ref[...] = res.astype(I32)


def _block_choice(q_rep, kmean):
    bsz, rows, _ = q_rep.shape
    n_blk = kmean.shape[1]
    return pl.pallas_call(
        _block_choice_kernel,
        grid=(bsz,),
        in_specs=[pl.BlockSpec((None, rows, A_WIDTH), lambda b: (b, 0, 0)),
                  pl.BlockSpec((None, n_blk, A_WIDTH), lambda b: (b, 0, 0))],
        out_specs=pl.BlockSpec((None, rows, LANES), lambda b: (b, 0, 0)),
        out_shape=jax.ShapeDtypeStruct((bsz, rows, LANES), I32),
        compiler_params=_params("parallel"),
        name="block_choice",
    )(q_rep, kmean)


def _moba_paged_kernel(pt_ref, blk_ref, slope_ref, q_ref, kn_ref, vn_ref, *refs,
                       t_new, page_size, past_len):
    del pt_ref
    o_ref = refs[-1]
    tiles_per_q = MOBA_TOPK * (MOBA_BLOCK // page_size)
    n_tiles = t_new * tiles_per_q
    k_tiles = refs[:n_tiles]
    v_tiles = refs[n_tiles:2 * n_tiles]
    b = pl.program_id(0)
    h = pl.program_id(1)
    scale = A_HEAD_DIM ** -0.5
    slope = slope_ref[...][:, :1]
    row = lax.broadcasted_iota(I32, (page_size, 1), 0)
    row_new = lax.broadcasted_iota(I32, (t_new, 1), 0)
    kn = kn_ref[...]
    vn = vn_ref[...]
    pages_per_block = MOBA_BLOCK // page_size
    for t in range(t_new):
        qt = q_ref[t:t + 1, :]
        cols = []
        for j in range(MOBA_TOPK):
            blk = blk_ref[((b * A_HEADS + h) * t_new + t) * MOBA_TOPK + j]
            for half in range(pages_per_block):
                kt = k_tiles[t * tiles_per_q + j * pages_per_block + half][...]
                lg = jnp.sum(kt * qt, axis=-1, keepdims=True) * scale
                dist = (past_len + t) - (blk * MOBA_BLOCK + half * page_size + row)
                cols.append(lg - slope * dist.astype(F32))
        dist_new = t - row_new
        lg_new = jnp.sum(kn * qt, axis=-1, keepdims=True) * scale - slope * dist_new.astype(F32)
        lg_new = jnp.where(dist_new >= 0, lg_new, NEG)
        mx = jnp.max(lg_new, axis=0, keepdims=True)
        for c in cols:
            mx = jnp.maximum(mx, jnp.max(c, axis=0, keepdims=True))
        p_new = jnp.exp(lg_new - mx)
        denom = jnp.sum(p_new, axis=0, keepdims=True)
        acc = jnp.sum(p_new * vn, axis=0, keepdims=True)
        for idx, c in enumerate(cols):
            p = jnp.exp(c - mx)
            denom = denom + jnp.sum(p, axis=0, keepdims=True)
            acc = acc + jnp.sum(p * v_tiles[t * tiles_per_q + idx][...], axis=0, keepdims=True)
        o_ref[t:t + 1, :] = acc / denom


def _moba_paged(qa, ka, va, pool_k, pool_v, page_table, blocks, slopes, page_size):
    bsz, t_new, _ = qa.shape
    n_pages = page_table.shape[1]
    past_len = n_pages * page_size
    pages_per_block = MOBA_BLOCK // page_size
    tiles_per_q = MOBA_TOPK * pages_per_block

    def tile_spec(t, j, half):
        def index(b, h, pt, blk):
            block = blk[((b * A_HEADS + h) * t_new + t) * MOBA_TOPK + j]
            return (pt[b, block * pages_per_block + half], 0, h)
        return pl.BlockSpec((None, page_size, A_HEAD_DIM), index)

    tile_specs = [tile_spec(t, j, half) for t in range(t_new) for j in range(MOBA_TOPK)
                  for half in range(pages_per_block)]
    new_spec = pl.BlockSpec((None, t_new, A_HEAD_DIM), lambda b, h, pt, blk: (b, 0, h))
    n_tiles = t_new * tiles_per_q
    return pl.pallas_call(
        functools.partial(_moba_paged_kernel, t_new=t_new, page_size=page_size, past_len=past_len),
        grid_spec=pltpu.PrefetchScalarGridSpec(
            num_scalar_prefetch=2,
            grid=(bsz, A_HEADS),
            in_specs=[pl.BlockSpec((None, 1, LANES), lambda b, h, pt, blk: (h, 0, 0)),
                      new_spec, new_spec, new_spec] + tile_specs + tile_specs,
            out_specs=new_spec),
        out_shape=jax.ShapeDtypeStruct((bsz, t_new, A_WIDTH), F32),
        compiler_params=_params("parallel", "arbitrary"),
        name="moba_paged",
    )(page_table, blocks, slopes, qa, ka, va, *([pool_k] * n_tiles), *([pool_v] * n_tiles))


def _ret_epilogue(o, g, nw):
    var = jnp.mean(o * o, axis=-1, keepdims=True)
    return (o * lax.rsqrt(var + EPS) * nw * _silu(g)).astype(BF16)


def _ret_chunk_kernel(q_ref, k_ref, v_ref, g_ref, din_ref, qd_ref, kd_ref, cd_ref, s0_ref, nw_ref,
                      o_ref, s_ref):
    @pl.when(pl.program_id(2) == 0)
    def _():
        s_ref[...] = s0_ref[...]

    k_t = (k_ref[...] * kd_ref[...]).T
    for hh in range(2):
        qb = q_ref[:, hh * R_QK_DIM:(hh + 1) * R_QK_DIM].astype(BF16)
        kb = k_ref[:, hh * R_QK_DIM:(hh + 1) * R_QK_DIM].astype(BF16)
        vb = v_ref[:, hh * R_V_DIM:(hh + 1) * R_V_DIM].astype(BF16)
        s = s_ref[hh]
        inner = lax.dot_general(qb, kb, NT_DIMS, preferred_element_type=F32) * din_ref[hh]
        o = (jnp.dot(inner.astype(BF16), vb, preferred_element_type=F32)
             + jnp.dot(qb, s.astype(BF16), preferred_element_type=F32) * qd_ref[hh])
        kdt = k_t[hh * R_QK_DIM:(hh + 1) * R_QK_DIM, :].astype(BF16)
        s_ref[hh] = s * cd_ref[hh] + jnp.dot(kdt, vb, preferred_element_type=F32)
        o_ref[:, hh * R_V_DIM:(hh + 1) * R_V_DIM] = _ret_epilogue(
            o, g_ref[:, hh * R_V_DIM:(hh + 1) * R_V_DIM], nw_ref[...])


def _ret_tables(chunk):
    log_gamma = jnp.log1p(-jnp.exp2(-5.0 - jnp.arange(R_HEADS, dtype=F32)))
    idx = jnp.arange(chunk, dtype=F32)
    diff = idx[:, None] - idx[None, :]
    causal = diff >= 0
    decay_in = jnp.where(causal[None], jnp.exp(jnp.where(causal, diff, 0.0)[None] * log_gamma[:, None, None]), 0.0)
    q_dec = jnp.exp((idx + 1)[None, :] * log_gamma[:, None])
    k_dec = jnp.exp((chunk - 1 - idx)[None, :] * log_gamma[:, None])
    chunk_dec = jnp.exp(chunk * log_gamma)
    return decay_in, q_dec, k_dec, chunk_dec


def _pair_lanes(k_dec):
    h, c = k_dec.shape
    t = jnp.broadcast_to(k_dec.reshape(h // 2, 2, c, 1), (h // 2, 2, c, R_QK_DIM))
    return t.transpose(0, 2, 1, 3).reshape(h // 2, c, 2 * R_QK_DIM)


def _retention_prompt(r, s0, ret_norm_w, batch, seq):
    c = math.gcd(seq, R_CHUNK)
    nc = seq // c
    decay_in, q_dec, k_dec, chunk_dec = _ret_tables(c)
    qd = q_dec[:, :, None]
    kd = _pair_lanes(k_dec)
    cd = jnp.broadcast_to(chunk_dec[:, None, None], (R_HEADS, 1, LANES))
    qk_blocks = R_QK_WIDTH // LANES
    v_off = 2 * R_QK_WIDTH // (2 * R_V_DIM)
    g_off = (2 * R_QK_WIDTH + R_WIDTH) // (2 * R_V_DIM)
    return pl.pallas_call(
        _ret_chunk_kernel,
        grid=(batch, R_HEADS // 2, nc),
        in_specs=[pl.BlockSpec((c, LANES), lambda b, p, i: (b * nc + i, p)),
                  pl.BlockSpec((c, LANES), lambda b, p, i: (b * nc + i, qk_blocks + p)),
                  pl.BlockSpec((c, 2 * R_V_DIM), lambda b, p, i: (b * nc + i, v_off + p)),
                  pl.BlockSpec((c, 2 * R_V_DIM), lambda b, p, i: (b * nc + i, g_off + p)),
                  pl.BlockSpec((2, c, c), lambda b, p, i: (p, 0, 0)),
                  pl.BlockSpec((2, c, 1), lambda b, p, i: (p, 0, 0)),
                  pl.BlockSpec((None, c, LANES), lambda b, p, i: (p, 0, 0)),
                  pl.BlockSpec((2, 1, LANES), lambda b, p, i: (p, 0, 0)),
                  pl.BlockSpec((None, 2, R_QK_DIM, R_V_DIM), lambda b, p, i: (b, p, 0, 0)),
                  pl.BlockSpec((1, R_V_DIM), lambda b, p, i: (0, 0))],
        out_specs=[pl.BlockSpec((c, 2 * R_V_DIM), lambda b, p, i: (b * nc + i, p)),
                   pl.BlockSpec((None, 2, R_QK_DIM, R_V_DIM), lambda b, p, i: (b, p, 0, 0))],
        out_shape=[jax.ShapeDtypeStruct((batch * seq, R_WIDTH), BF16),
                   jax.ShapeDtypeStruct((batch, R_HEADS, R_QK_DIM, R_V_DIM), F32)],
        compiler_params=_params("parallel", "parallel", "arbitrary"),
        name="retention_chunks",
    )(r, r, r, r, decay_in, qd, kd, cd, s0, ret_norm_w)


def _ret_step_kernel(q_ref, k_ref, v_ref, g_ref, din_ref, qd_ref, kd_ref, cd_ref, s0_ref, nw_ref,
                     o_ref, s_ref, *, t_new):
    rows = q_ref.shape[0]
    bsz = rows // t_new
    k_t = (k_ref[...] * kd_ref[...]).T
    seq_of_row = lax.broadcasted_iota(I32, (rows, bsz * R_QK_DIM), 0) // t_new
    seq_of_col = lax.broadcasted_iota(I32, (rows, bsz * R_QK_DIM), 1) // R_QK_DIM
    own_q = seq_of_row == seq_of_col
    seq_of_srow = lax.broadcasted_iota(I32, (bsz * R_QK_DIM, rows), 0) // R_QK_DIM
    seq_of_scol = lax.broadcasted_iota(I32, (bsz * R_QK_DIM, rows), 1) // t_new
    own_k = seq_of_srow == seq_of_scol
    for hh in range(2):
        q = q_ref[:, hh * R_QK_DIM:(hh + 1) * R_QK_DIM]
        qb = q.astype(BF16)
        kb = k_ref[:, hh * R_QK_DIM:(hh + 1) * R_QK_DIM].astype(BF16)
        vb = v_ref[:, hh * R_V_DIM:(hh + 1) * R_V_DIM].astype(BF16)
        s = s0_ref[:, hh].reshape(bsz * R_QK_DIM, R_V_DIM)
        inner = lax.dot_general(qb, kb, NT_DIMS, preferred_element_type=F32) * din_ref[hh]
        q_wide = jnp.where(own_q, jnp.concatenate([q] * bsz, axis=1), 0.0).astype(BF16)
        o = (jnp.dot(inner.astype(BF16), vb, preferred_element_type=F32)
             + jnp.dot(q_wide, s.astype(BF16), preferred_element_type=F32) * qd_ref[hh])
        kdt = k_t[hh * R_QK_DIM:(hh + 1) * R_QK_DIM, :]
        k_tall = jnp.where(own_k, jnp.concatenate([kdt] * bsz, axis=0), 0.0).astype(BF16)
        s_new = s * cd_ref[hh] + jnp.dot(k_tall, vb, preferred_element_type=F32)
        s_ref[:, hh] = s_new.reshape(bsz, R_QK_DIM, R_V_DIM)
        o_ref[:, hh * R_V_DIM:(hh + 1) * R_V_DIM] = _ret_epilogue(
            o, g_ref[:, hh * R_V_DIM:(hh + 1) * R_V_DIM], nw_ref[...])


def _retention_step(r, s0, ret_norm_w, batch, t_new):
    rows = batch * t_new
    decay_in, q_dec, k_dec, chunk_dec = _ret_tables(t_new)
    same_seq = jnp.eye(batch, dtype=F32)
    din = jnp.einsum("hij,ab->haibj", decay_in, same_seq).reshape(R_HEADS, rows, rows)
    qd = jnp.tile(q_dec, (1, batch))[:, :, None]
    kd = _pair_lanes(jnp.tile(k_dec, (1, batch)))
    cd = jnp.broadcast_to(chunk_dec[:, None, None], (R_HEADS, 1, LANES))
    qk_blocks = R_QK_WIDTH // LANES
    v_off = 2 * R_QK_WIDTH // (2 * R_V_DIM)
    g_off = (2 * R_QK_WIDTH + R_WIDTH) // (2 * R_V_DIM)
    return pl.pallas_call(
        functools.partial(_ret_step_kernel, t_new=t_new),
        grid=(R_HEADS // 2,),
        in_specs=[pl.BlockSpec((rows, LANES), lambda p: (0, p)),
                  pl.BlockSpec((rows, LANES), lambda p: (0, qk_blocks + p)),
                  pl.BlockSpec((rows, 2 * R_V_DIM), lambda p: (0, v_off + p)),
                  pl.BlockSpec((rows, 2 * R_V_DIM), lambda p: (0, g_off + p)),
                  pl.BlockSpec((2, rows, rows), lambda p: (p, 0, 0)),
                  pl.BlockSpec((2, rows, 1), lambda p: (p, 0, 0)),
                  pl.BlockSpec((None, rows, LANES), lambda p: (p, 0, 0)),
                  pl.BlockSpec((2, 1, LANES), lambda p: (p, 0, 0)),
                  pl.BlockSpec((batch, 2, R_QK_DIM, R_V_DIM), lambda p: (0, p, 0, 0)),
                  pl.BlockSpec((1, R_V_DIM), lambda p: (0, 0))],
        out_specs=[pl.BlockSpec((rows, 2 * R_V_DIM), lambda p: (0, p)),
                   pl.BlockSpec((batch, 2, R_QK_DIM, R_V_DIM), lambda p: (0, p, 0, 0))],
        out_shape=[jax.ShapeDtypeStruct((rows, R_WIDTH), BF16),
                   jax.ShapeDtypeStruct((batch, R_HEADS, R_QK_DIM, R_V_DIM), F32)],
        compiler_params=_params("parallel"),
        name="retention_step",
    )(r, r, r, r, din, qd, kd, cd, s0, ret_norm_w)


def _outproj_kernel(a_ref, r_ref, x_ref, g1_ref, sh_ref, sc_ref, nw_ref, wa_ref, wr_ref,
                    x1_ref, h2_ref):
    mixed = (jnp.dot(a_ref[...].astype(BF16), wa_ref[...], preferred_element_type=F32)
             + jnp.dot(r_ref[...].astype(BF16), wr_ref[...], preferred_element_type=F32))
    x1 = x_ref[...] + g1_ref[...] * mixed
    x1_ref[...] = x1
    var = jnp.mean(x1 * x1, axis=-1, keepdims=True)
    h2 = x1 * lax.rsqrt(var + EPS) * nw_ref[...]
    h2_ref[...] = (h2 * (1 + sc_ref[...]) + sh_ref[...]).astype(BF16)


def _outproj(out_a, out_r, x, gate, shift, scale, norm_w, w_out, tm, tiles_per_mod):
    m, d = x.shape
    mod_rows = shift.shape[1]
    mod_spec = pl.BlockSpec((None, mod_rows, d), lambda i: (i // tiles_per_mod, 0, 0))
    return pl.pallas_call(
        _outproj_kernel,
        grid=(m // tm,),
        in_specs=[pl.BlockSpec((tm, A_WIDTH), lambda i: (i, 0)),
                  pl.BlockSpec((tm, R_WIDTH), lambda i: (i, 0)),
                  pl.BlockSpec((tm, d), lambda i: (i, 0)),
                  mod_spec, mod_spec, mod_spec,
                  pl.BlockSpec((1, d), lambda i: (0, 0)),
                  pl.BlockSpec((A_WIDTH, d), lambda i: (0, 0)),
                  pl.BlockSpec((R_WIDTH, d), lambda i: (A_WIDTH // R_WIDTH, 0))],
        out_specs=[pl.BlockSpec((tm, d), lambda i: (i, 0)),
                   pl.BlockSpec((tm, d), lambda i: (i, 0))],
        out_shape=[jax.ShapeDtypeStruct((m, d), F32), jax.ShapeDtypeStruct((m, d), BF16)],
        compiler_params=_params("parallel"),
        name="outproj",
    )(out_a, out_r, x, gate, shift, scale, norm_w, w_out, w_out)


def _pruned_pairs():
    return [(a, P_TOPK // (a + 1)) for a in range(P_TOPK)]


N_CAND = sum(nb for _, nb in _pruned_pairs())
CAND_ROWS = -(-N_CAND // 8) * 8


def _top_rows(x, k, val_ref, idx_ref):
    nrows = x.shape[0]
    row = lax.broadcasted_iota(I32, x.shape, 0).astype(F32)
    for i in range(k):
        mx = jnp.max(x, axis=0, keepdims=True)
        am = jnp.min(jnp.where(x == mx, row, float(nrows)), axis=0, keepdims=True)
        val_ref[i:i + 1, :] = mx
        idx_ref[i:i + 1, :] = am
        x = jnp.where(row == am, -jnp.inf, x)


def _peer_route_kernel(h_ref, wq_ref, sk_ref, i1_ref, i2_ref, g_ref,
                       s1_ref, k1_ref, s2_ref, k2_ref, cand_ref, c1_ref, c2_ref):
    tm = h_ref.shape[0]
    qp = jnp.dot(h_ref[...], wq_ref[...], preferred_element_type=F32)
    cand_ref[...] = jnp.full(cand_ref.shape, -jnp.inf, F32)
    c1_ref[...] = jnp.zeros(c1_ref.shape, F32)
    c2_ref[...] = jnp.zeros(c2_ref.shape, F32)
    row = lax.broadcasted_iota(I32, (CAND_ROWS, tm), 0).astype(F32)
    for hd in range(P_HEADS):
        for half, (s_ref, k_ref) in enumerate(((s1_ref, k1_ref), (s2_ref, k2_ref))):
            col = (hd * 2 + half) * LANES
            q_part = qp[:, col:col + LANES].astype(BF16)
            scores = lax.dot_general(sk_ref[hd * 2 + half], q_part, NT_DIMS,
                                     preferred_element_type=F32)
            _top_rows(scores, P_TOPK, s_ref, k_ref)
        off = 0
        for a, nb in _pruned_pairs():
            cand_ref[off:off + nb, :] = s1_ref[a:a + 1, :] + s2_ref[0:nb, :]
            c1_ref[off:off + nb, :] = jnp.broadcast_to(k1_ref[a:a + 1, :], (nb, tm))
            c2_ref[off:off + nb, :] = k2_ref[0:nb, :]
            off += nb
        cand = cand_ref[...]
        c1 = c1_ref[...]
        c2 = c2_ref[...]
        best, e1, e2 = [], [], []
        for _ in range(P_TOPK):
            mx = jnp.max(cand, axis=0, keepdims=True)
            am = jnp.min(jnp.where(cand == mx, row, float(CAND_ROWS)), axis=0, keepdims=True)
            hit = row == am
            best.append(mx)
            e1.append(jnp.max(jnp.where(hit, c1, -1.0), axis=0, keepdims=True))
            e2.append(jnp.max(jnp.where(hit, c2, -1.0), axis=0, keepdims=True))
            cand = jnp.where(hit, -jnp.inf, cand)
        top = best[0]
        exps = [jnp.exp(v - top) for v in best]
        denom = exps[0]
        for v in exps[1:]:
            denom = denom + v
        for i in range(P_TOPK):
            slot = hd * P_TOPK + i
            i1_ref[slot:slot + 1, :] = e1[i].astype(I32)
            i2_ref[slot:slot + 1, :] = e2[i].astype(I32)
            g_ref[slot:slot + 1, :] = exps[i] / denom


def _peer_route(h2, w_query, sub_keys, tm):
    m, d = h2.shape
    qd = w_query.shape[1]
    slot_spec = pl.BlockSpec((P_SLOTS, tm), lambda i: (0, i))
    return pl.pallas_call(
        _peer_route_kernel,
        grid=(m // tm,),
        in_specs=[pl.BlockSpec((tm, d), lambda i: (i, 0)),
                  pl.BlockSpec((d, qd), lambda i: (0, 0)),
                  pl.BlockSpec(sub_keys.shape, lambda i: (0, 0, 0))],
        out_specs=[slot_spec, slot_spec, slot_spec],
        out_shape=[jax.ShapeDtypeStruct((P_SLOTS, m), I32),
                   jax.ShapeDtypeStruct((P_SLOTS, m), I32),
                   jax.ShapeDtypeStruct((P_SLOTS, m), F32)],
        scratch_shapes=[pltpu.VMEM((P_TOPK, tm), F32)] * 4 + [pltpu.VMEM((CAND_ROWS, tm), F32)] * 3,
        compiler_params=_params("parallel"),
        name="peer_route",
    )(h2, w_query, sub_keys)


def _peer_mask_kernel(i1_ref, i2_ref, g_ref, o_ref):
    tt = i1_ref.shape[0]
    key = lax.broadcasted_iota(I32, (P_NKEYS, P_SLOTS), 0)

    def body(t, carry):
        i1 = i1_ref[pl.ds(t, 1), :]
        i2 = i2_ref[pl.ds(t, 1), :]
        g = g_ref[pl.ds(t, 1), :]
        a = jnp.where(key == i1, g, 0.0).astype(BF16)
        bm = jnp.where(key == i2, 1.0, 0.0).astype(BF16)
        o_ref[t] = lax.dot_general(a, bm, NT_DIMS, preferred_element_type=F32).astype(o_ref.dtype)
        return carry

    lax.fori_loop(0, tt, body, 0)


def _peer_mask(i1, i2, gates, tt):
    m = i1.shape[0]
    slot_spec = pl.BlockSpec((tt, P_SLOTS), lambda i: (i, 0))
    return pl.pallas_call(
        _peer_mask_kernel,
        grid=(m // tt,),
        in_specs=[slot_spec, slot_spec, slot_spec],
        out_specs=pl.BlockSpec((tt, P_NKEYS, P_NKEYS), lambda i: (i, 0, 0)),
        out_shape=jax.ShapeDtypeStruct((m, P_NKEYS, P_NKEYS), BF16),
        compiler_params=_params("parallel"),
        name="peer_mask",
    )(i1, i2, gates)


def _peer_mlp_kernel(h_ref, dn_ref, up_ref, m_ref, x_ref, g2_ref, o_ref, acc_ref):
    e = pl.program_id(1)

    @pl.when(e == 0)
    def _():
        acc_ref[...] = jnp.zeros(acc_ref.shape, F32)

    pre = lax.dot_general(h_ref[...], dn_ref[...], NT_DIMS, preferred_element_type=F32)
    act = 0.5 * pre * (1.0 + lax.erf(pre * (0.5 ** 0.5)))
    w = (act * m_ref[...].astype(F32)).astype(BF16)
    acc_ref[...] += jnp.dot(w, up_ref[...], preferred_element_type=F32)

    @pl.when(e == pl.num_programs(1) - 1)
    def _():
        o_ref[...] = x_ref[...] + g2_ref[...] * acc_ref[...]


def _peer_mlp(h2, down, up, mask, x1, gate, tm, te, tiles_per_mod):
    m, d = h2.shape
    n_exp = down.shape[0]
    mod_rows = gate.shape[1]
    return pl.pallas_call(
        _peer_mlp_kernel,
        grid=(m // tm, n_exp // te),
        in_specs=[pl.BlockSpec((tm, d), lambda i, e: (i, 0)),
                  pl.BlockSpec((te, d), lambda i, e: (e, 0)),
                  pl.BlockSpec((te, d), lambda i, e: (e, 0)),
                  pl.BlockSpec((tm, te), lambda i, e: (i, e)),
                  pl.BlockSpec((tm, d), lambda i, e: (i, 0)),
                  pl.BlockSpec((None, mod_rows, d), lambda i, e: (i // tiles_per_mod, 0, 0))],
        out_specs=pl.BlockSpec((tm, d), lambda i, e: (i, 0)),
        out_shape=jax.ShapeDtypeStruct((m, d), F32),
        scratch_shapes=[pltpu.VMEM((tm, d), F32)],
        compiler_params=_params("parallel", "arbitrary"),
        name="peer_mlp",
    )(h2, down, up, mask, x1, gate)


def _alibi_slope_rows():
    slopes = jnp.exp2(-8.0 * jnp.arange(1, A_HEADS + 1, dtype=F32) / A_HEADS)
    return jnp.broadcast_to(slopes[:, None, None], (A_HEADS, 1, LANES))


def _layer(x, mod, pool_k, pool_v, page_table, s0, w, *, tm, tm_mlp):
    bsz, seq, d = x.shape
    m = bsz * seq
    xf = x.reshape(m, d)
    chunks = jnp.split(mod, 6, axis=-1)
    if seq % tm == 0:
        mods = [c[:, None, :] for c in chunks]
        tiles_per_mod = seq // tm
        tiles_per_mod_mlp = seq // tm_mlp
    else:
        assert m == tm == tm_mlp
        mods = [jnp.repeat(c, seq, axis=0)[None] for c in chunks]
        tiles_per_mod = tiles_per_mod_mlp = 1
    sh1, sc1, g1, sh2, sc2, g2 = mods

    proj = functools.partial(_inproj, xf, w["norm1_w"], sh1, sc1, w["w_in"], tm=tm,
                             tiles_per_mod=tiles_per_mod)
    qa = proj(col0=0, ncols=A_WIDTH, aux=w["q_norm_w"], mode="headnorm")
    ka = proj(col0=A_WIDTH, ncols=A_WIDTH, aux=w["k_norm_w"], mode="headnorm")
    va = proj(col0=2 * A_WIDTH, ncols=A_WIDTH, aux=w["ones_a"], mode="scale")
    r = proj(col0=3 * A_WIDTH, ncols=2 * R_QK_WIDTH + 2 * R_WIDTH, aux=w["r_scale"], mode="scale")

    slopes = _alibi_slope_rows()
    if pool_k is None:
        out_a = _moba_prompt(qa, ka, va, slopes, bsz, seq)
        out_r, s_new = _retention_prompt(r, s0, w["ret_norm_w"], bsz, seq)
    else:
        page_size = pool_k.shape[1]
        kmean = _page_means(pool_k, page_table, page_size)
        q3 = qa.reshape(bsz, seq, A_WIDTH)
        q_rep = jnp.broadcast_to(q3[:, None], (bsz, A_HEADS, seq, A_WIDTH)).reshape(bsz, A_HEADS * seq, A_WIDTH)
        choice = _block_choice(q_rep, kmean.reshape(bsz, kmean.shape[1], A_WIDTH))
        blocks = choice[:, :, :MOBA_TOPK].reshape(-1)
        n_pool = pool_k.shape[0]
        out_a = _moba_paged(q3, ka.reshape(bsz, seq, A_WIDTH), va.reshape(bsz, seq, A_WIDTH),
                            pool_k.reshape(n_pool, page_size, A_WIDTH),
                            pool_v.reshape(n_pool, page_size, A_WIDTH),
                            page_table, blocks, slopes, page_size).reshape(m, A_WIDTH)
        out_r, s_new = _retention_step(r, s0, w["ret_norm_w"], bsz, seq)

    tm_small = min(tm, 256)
    x1, h2 = _outproj(out_a, out_r, xf, g1, sh2, sc2, w["norm2_w"], w["w_out"], tm_small,
                      tiles_per_mod * (tm // tm_small))

    i1, i2, gates = _peer_route(h2, w["peer_w_query"], w["peer_sub_keys"], tm=tm_small)
    mask = _peer_mask(i1.T, i2.T, gates.T, tt=min(m, 64))
    y = _peer_mlp(h2, w["peer_down"], w["peer_up"], mask.reshape(m, P_NKEYS * P_NKEYS), x1, g2,
                  tm=tm_mlp, te=512, tiles_per_mod=tiles_per_mod_mlp)
    k_out = ka.reshape(bsz, seq, A_HEADS, A_HEAD_DIM)
    v_out = va.reshape(bsz, seq, A_HEADS, A_HEAD_DIM)
    return y.reshape(bsz, seq, d), k_out, v_out, s_new


def kernel(x_prompt, x_sample, cache_k, cache_v, state_ret, page_table, c_prompt, c_sample, w_ada, b_ada, norm1_w, w_in, q_norm_w, k_norm_w, ret_norm_w, w_out, norm2_w, peer_w_query, peer_sub_keys, peer_down, peer_up):
    depth = w_ada.shape[0]
    n_prompt = c_prompt.shape[0]
    n_sample = c_sample.shape[0]
    pad = (-(n_prompt + n_sample)) % 8
    d = x_prompt.shape[-1]
    hp, hs = x_prompt, x_sample
    s0_prompt = jnp.zeros((n_prompt, R_HEADS, R_QK_DIM, R_V_DIM), F32)
    r_scale = jnp.concatenate([jnp.ones((1, R_QK_WIDTH), F32),
                               jnp.full((1, R_QK_WIDTH), R_QK_DIM ** -0.5, F32),
                               jnp.ones((1, 2 * R_WIDTH), F32)], axis=1)
    outs = [[] for _ in range(6)]
    for l in range(depth):
        c_all = jnp.concatenate([c_prompt, c_sample, jnp.zeros((pad, d), F32)], axis=0)
        mod = _ada(c_all, w_ada[l], b_ada[l][None])
        w = dict(norm1_w=norm1_w[l][None], w_in=w_in[l].astype(BF16), q_norm_w=q_norm_w[l][None],
                 k_norm_w=k_norm_w[l][None], ret_norm_w=ret_norm_w[l][None],
                 w_out=w_out[l].astype(BF16), norm2_w=norm2_w[l][None],
                 peer_w_query=peer_w_query[l].astype(BF16),
                 peer_sub_keys=peer_sub_keys[l].reshape(2 * P_HEADS, P_NKEYS, -1).astype(BF16),
                 peer_down=peer_down[l].astype(BF16), peer_up=peer_up[l].astype(BF16),
                 ones_a=jnp.ones((1, A_WIDTH), F32), r_scale=r_scale)
        hp, k1, v1, s1 = _layer(hp, mod[:n_prompt], None, None, None, s0_prompt, w, tm=512, tm_mlp=512)
        hs, k2, v2, s2 = _layer(hs, mod[n_prompt:n_prompt + n_sample], cache_k[l], cache_v[l], page_table,
                                state_ret[l], w, tm=hs.shape[0] * hs.shape[1], tm_mlp=hs.shape[0] * hs.shape[1])
        for lst, val in zip(outs, (k1, v1, s1, k2, v2, s2)):
            lst.append(val)
    return (hp, hs) + tuple(jnp.stack(o) for o in outs)
```

```python
import functools
import math

import jax
import jax.numpy as jnp
from jax import lax
from jax.experimental import pallas as pl
from jax.experimental.pallas import tpu as pltpu

F32 = jnp.float32
BF16 = jnp.bfloat16
I32 = jnp.int32

EPS = 1e-6
NEG = -1e30

A_HEADS = 8
A_HEAD_DIM = 128
A_WIDTH = A_HEADS * A_HEAD_DIM
MOBA_BLOCK = 256
MOBA_TOPK = 3
R_HEADS = 8
R_QK_DIM = 64
R_V_DIM = 128
R_QK_WIDTH = R_HEADS * R_QK_DIM
R_WIDTH = R_HEADS * R_V_DIM
R_CHUNK = 128
P_HEADS = 8
P_NKEYS = 128
P_TOPK = 16
P_SLOTS = P_HEADS * P_TOPK

LANES = 128
VMEM_LIMIT_BYTES = 48 * 1024 * 1024

NT_DIMS = (((1,), (1,)), ((), ()))


def _params(*semantics):
    return pltpu.CompilerParams(dimension_semantics=semantics, vmem_limit_bytes=VMEM_LIMIT_BYTES)


def _silu(x):
    return x * jax.nn.sigmoid(x)


def _ada_kernel(c_ref, w_ref, b_ref, o_ref):
    s = _silu(c_ref[...]).astype(BF16)
    o_ref[...] = jnp.dot(s, w_ref[...].astype(BF16), preferred_element_type=F32) + b_ref[...]


def _ada(c, w, b):
    rows, d = c.shape
    n = w.shape[1]
    tn = 1024
    return pl.pallas_call(
        _ada_kernel,
        grid=(n // tn,),
        in_specs=[pl.BlockSpec((rows, d), lambda j: (0, 0)),
                  pl.BlockSpec((d, tn), lambda j: (0, j)),
                  pl.BlockSpec((1, tn), lambda j: (0, j))],
        out_specs=pl.BlockSpec((rows, tn), lambda j: (0, j)),
        out_shape=jax.ShapeDtypeStruct((rows, n), F32),
        compiler_params=_params("parallel"),
        name="ada",
    )(c, w, b)


def _inproj_kernel(x_ref, nw_ref, sh_ref, sc_ref, w_ref, aux_ref, o_ref, *, mode):
    x = x_ref[...]
    var = jnp.mean(x * x, axis=-1, keepdims=True)
    h = x * lax.rsqrt(var + EPS) * nw_ref[...]
    h = h * (1 + sc_ref[...]) + sh_ref[...]
    y = jnp.dot(h.astype(BF16), w_ref[...], preferred_element_type=F32)
    if mode == "headnorm":
        for hd in range(y.shape[1] // A_HEAD_DIM):
            yh = y[:, hd * A_HEAD_DIM:(hd + 1) * A_HEAD_DIM]
            v = jnp.mean(yh * yh, axis=-1, keepdims=True)
            o_ref[:, hd * A_HEAD_DIM:(hd + 1) * A_HEAD_DIM] = yh * lax.rsqrt(v + EPS) * aux_ref[...]
    elif mode == "scale":
        o_ref[...] = y * aux_ref[...]
    else:
        o_ref[...] = y


def _inproj(x, norm_w, shift, scale, w, col0, ncols, aux, mode, tm, tiles_per_mod):
    m, d = x.shape
    tn = 1024
    assert ncols % tn == 0 and col0 % tn == 0 and m % tm == 0
    mod_rows = shift.shape[1]
    mod_spec = pl.BlockSpec((None, mod_rows, d), lambda i, j: (i // tiles_per_mod, 0, 0))
    if mode == "headnorm":
        aux_spec = pl.BlockSpec((1, A_HEAD_DIM), lambda i, j: (0, 0))
    else:
        aux_spec = pl.BlockSpec((1, tn), lambda i, j: (0, j))
    return pl.pallas_call(
        functools.partial(_inproj_kernel, mode=mode),
        grid=(m // tm, ncols // tn),
        in_specs=[pl.BlockSpec((tm, d), lambda i, j: (i, 0)),
                  pl.BlockSpec((1, d), lambda i, j: (0, 0)),
                  mod_spec, mod_spec,
                  pl.BlockSpec((d, tn), lambda i, j: (0, col0 // tn + j)),
                  aux_spec],
        out_specs=pl.BlockSpec((tm, tn), lambda i, j: (i, j)),
        out_shape=jax.ShapeDtypeStruct((m, ncols), F32),
        compiler_params=_params("parallel", "arbitrary"),
        name="inproj_" + mode,
    )(x, norm_w, shift, scale, w, aux)


def _moba_prompt_kernel(slope_ref, q_ref, k_ref, v_ref, o_ref, km_ref, vt_ref, *, seq):
    qi = pl.program_id(2)
    blk = MOBA_BLOCK
    nb = seq // blk

    @pl.when(qi == 0)
    def _():
        km_ref[...] = jnp.mean(k_ref[...].reshape(nb, blk, A_HEAD_DIM), axis=1)
        vt_ref[...] = v_ref[...].T.astype(BF16)

    q_t = q_ref[...].T
    gate = jnp.dot(km_ref[...], q_t, precision=lax.Precision.HIGHEST,
                   preferred_element_type=F32)
    g = [jnp.where(n < qi, gate[n:n + 1, :], NEG) for n in range(nb)]
    scale = A_HEAD_DIM ** -0.5
    slope = slope_ref[...][:, :1]
    col_terms = []
    for n in range(nb):
        cnt = jnp.zeros((1, blk), F32)
        for m in range(nb):
            if m < n:
                cnt = cnt + jnp.where(g[m] >= g[n], 1.0, 0.0)
            elif m > n:
                cnt = cnt + jnp.where(g[m] > g[n], 1.0, 0.0)
        chosen = jnp.where(n < qi, jnp.where(cnt < MOBA_TOPK, 1.0, 0.0), jnp.where(n == qi, 1.0, 0.0))
        ahead = ((qi - n) * blk).astype(F32)
        col_terms.append(jnp.where(chosen > 0.5, -slope * ahead, NEG))

    rc = (lax.broadcasted_iota(I32, (blk, blk), 1) - lax.broadcasted_iota(I32, (blk, blk), 0))
    bias_past = -slope * rc.astype(F32)
    bias_own = jnp.where(rc >= 0, bias_past, NEG)
    q_tb = q_t.astype(BF16)
    pieces = []
    for n in range(nb):
        kb = k_ref[n * blk:(n + 1) * blk, :].astype(BF16)
        s_t = jnp.dot(kb, q_tb, preferred_element_type=F32)
        pieces.append(s_t * scale + jnp.where(n == qi, bias_own, bias_past) + col_terms[n])
    logits = jnp.concatenate(pieces, axis=0)
    mx = jnp.max(logits, axis=0, keepdims=True)
    p = jnp.exp(logits - mx)
    denom = jnp.sum(p, axis=0, keepdims=True)
    out_t = jnp.dot(vt_ref[...], p.astype(BF16), preferred_element_type=F32)
    o_ref[...] = (out_t / denom).T.astype(o_ref.dtype)


def _moba_prompt(qa, ka, va, slopes, batch, seq):
    blk = MOBA_BLOCK
    nq = seq // blk
    return pl.pallas_call(
        functools.partial(_moba_prompt_kernel, seq=seq),
        grid=(batch, A_HEADS, nq),
        in_specs=[pl.BlockSpec((None, 1, LANES), lambda b, h, i: (h, 0, 0)),
                  pl.BlockSpec((blk, A_HEAD_DIM), lambda b, h, i: (b * nq + i, h)),
                  pl.BlockSpec((seq, A_HEAD_DIM), lambda b, h, i: (b, h)),
                  pl.BlockSpec((seq, A_HEAD_DIM), lambda b, h, i: (b, h))],
        out_specs=pl.BlockSpec((blk, A_HEAD_DIM), lambda b, h, i: (b * nq + i, h)),
        out_shape=jax.ShapeDtypeStruct((batch * seq, A_WIDTH), BF16),
        scratch_shapes=[pltpu.VMEM((nq, A_HEAD_DIM), F32),
                        pltpu.VMEM((A_HEAD_DIM, seq), BF16)],
        compiler_params=_params("parallel", "parallel", "arbitrary"),
        name="moba_prompt",
    )(slopes, qa, ka, va)


PAGES_PER_STEP = 8


def _page_mean_kernel(pt_ref, *refs):
    del pt_ref
    o_ref = refs[-1]
    pages_per_block = PAGES_PER_STEP // o_ref.shape[0]
    for j in range(o_ref.shape[0]):
        acc = None
        rows = 0
        for u in range(pages_per_block):
            page = refs[j * pages_per_block + u]
            part = jnp.sum(page[...], axis=0)
            rows += page.shape[0]
            acc = part if acc is None else acc + part
        o_ref[j] = acc * (1.0 / rows)


def _page_means(pool, layer, page_table):
    page_size = pool.shape[2]
    bsz, n_pages = page_table.shape
    pages_per_block = MOBA_BLOCK // page_size
    blocks_per_step = PAGES_PER_STEP // pages_per_block
    n_blk = n_pages // pages_per_block
    assert n_pages % PAGES_PER_STEP == 0

    def page_spec(u):
        return pl.BlockSpec((None, None, page_size, A_HEADS, A_HEAD_DIM),
                            lambda b, g, pt: (layer, pt[b, g * PAGES_PER_STEP + u], 0, 0, 0))

    return pl.pallas_call(
        _page_mean_kernel,
        grid_spec=pltpu.PrefetchScalarGridSpec(
            num_scalar_prefetch=1,
            grid=(bsz, n_pages // PAGES_PER_STEP),
            in_specs=[page_spec(u) for u in range(PAGES_PER_STEP)],
            out_specs=pl.BlockSpec((None, blocks_per_step, A_HEADS, A_HEAD_DIM),
                                   lambda b, g, pt: (b, g, 0, 0))),
        out_shape=jax.ShapeDtypeStruct((bsz, n_blk, A_HEADS, A_HEAD_DIM), F32),
        compiler_params=_params("parallel", "arbitrary"),
        name="page_means",
    )(page_table, *([pool] * PAGES_PER_STEP))


def _block_choice_kernel(q_ref, km_ref, o_ref):
    rows = q_ref.shape[0]
    t_new = rows // A_HEADS
    n_blk = km_ref.shape[0]
    q = q_ref[...]
    r_head = lax.broadcasted_iota(I32, q.shape, 0) // t_new
    c_head = lax.broadcasted_iota(I32, q.shape, 1) // A_HEAD_DIM
    qbd = jnp.where(r_head == c_head, q, 0.0)
    gate = lax.dot_general(qbd, km_ref[...], NT_DIMS, precision=lax.Precision.HIGHEST,
                           preferred_element_type=F32)
    lane = lax.broadcasted_iota(I32, gate.shape, 1).astype(F32)
    out_lane = lax.broadcasted_iota(I32, (rows, LANES), 1)
    res = jnp.zeros((rows, LANES), F32)
    for it in range(MOBA_TOPK):
        mx = jnp.max(gate, axis=1, keepdims=True)
        idx = jnp.min(jnp.where(gate == mx, lane, float(n_blk)), axis=1, keepdims=True)
        res = jnp.where(out_lane == it, idx, res)
        gate = jnp.where(lane == idx, -jnp.inf, gate)
    o_ref[...] = res.astype(I32)


def _block_choice(q_rep, kmean):
    bsz, rows, _ = q_rep.shape
    n_blk = kmean.shape[1]
    return pl.pallas_call(
        _block_choice_kernel,
        grid=(bsz,),
        in_specs=[pl.BlockSpec((None, rows, A_WIDTH), lambda b: (b, 0, 0)),
                  pl.BlockSpec((None, n_blk, A_WIDTH), lambda b: (b, 0, 0))],
        out_specs=pl.BlockSpec((None, rows, LANES), lambda b: (b, 0, 0)),
        out_shape=jax.ShapeDtypeStruct((bsz, rows, LANES), I32),
        compiler_params=_params("parallel"),
        name="block_choice",
    )(q_rep, kmean)


def _moba_paged_kernel(pt_ref, blk_ref, slope_ref, q_ref, kn_ref, vn_ref, pk_ref, pv_ref, o_ref,
                       kbuf, vbuf, sem, *, layer, t_new, page_size, past_len):
    pages_per_block = MOBA_BLOCK // page_size
    tiles_per_q = MOBA_TOPK * pages_per_block
    n_heads = pl.num_programs(1)
    step = pl.program_id(0) * n_heads + pl.program_id(1)
    n_steps = pl.num_programs(0) * n_heads
    slot = step % 2

    def chosen_block(step_i, t, j):
        return blk_ref[(step_i * t_new + t) * MOBA_TOPK + j]

    def tile_copies(step_i, slot_i):
        b = step_i // n_heads
        h = step_i % n_heads
        copies = []
        for t in range(t_new):
            for j in range(MOBA_TOPK):
                block = chosen_block(step_i, t, j)
                for half in range(pages_per_block):
                    page = pt_ref[b, block * pages_per_block + half]
                    idx = t * tiles_per_q + j * pages_per_block + half
                    copies.append(pltpu.make_async_copy(pk_ref.at[layer, page, :, h, :],
                                                        kbuf.at[slot_i, idx], sem.at[slot_i]))
                    copies.append(pltpu.make_async_copy(pv_ref.at[layer, page, :, h, :],
                                                        vbuf.at[slot_i, idx], sem.at[slot_i]))
        return copies

    @pl.when(step == 0)
    def _():
        for c in tile_copies(step, slot):
            c.start()

    @pl.when(step + 1 < n_steps)
    def _():
        for c in tile_copies(step + 1, 1 - slot):
            c.start()

    for c in tile_copies(step, slot):
        c.wait()

    scale = A_HEAD_DIM ** -0.5
    slope = slope_ref[...][:, :1]
    row = lax.broadcasted_iota(I32, (page_size, 1), 0)
    row_new = lax.broadcasted_iota(I32, (t_new, 1), 0)
    kn = kn_ref[...]
    vn = vn_ref[...]
    for t in range(t_new):
        qt = q_ref[t:t + 1, :]
        cols = []
        for j in range(MOBA_TOPK):
            block = chosen_block(step, t, j)
            for half in range(pages_per_block):
                kt = kbuf[slot, t * tiles_per_q + j * pages_per_block + half]
                lg = jnp.sum(kt * qt, axis=-1, keepdims=True) * scale
                dist = (past_len + t) - (block * MOBA_BLOCK + half * page_size + row)
                cols.append(lg - slope * dist.astype(F32))
        dist_new = t - row_new
        lg_new = jnp.sum(kn * qt, axis=-1, keepdims=True) * scale - slope * dist_new.astype(F32)
        lg_new = jnp.where(dist_new >= 0, lg_new, NEG)
        mx = jnp.max(lg_new, axis=0, keepdims=True)
        for c in cols:
            mx = jnp.maximum(mx, jnp.max(c, axis=0, keepdims=True))
        p_new = jnp.exp(lg_new - mx)
        denom = jnp.sum(p_new, axis=0, keepdims=True)
        acc = jnp.sum(p_new * vn, axis=0, keepdims=True)
        for idx, c in enumerate(cols):
            p = jnp.exp(c - mx)
            denom = denom + jnp.sum(p, axis=0, keepdims=True)
            acc = acc + jnp.sum(p * vbuf[slot, t * tiles_per_q + idx], axis=0, keepdims=True)
        o_ref[t:t + 1, :] = acc / denom


def _moba_paged(qa, ka, va, pool_k, pool_v, layer, page_table, blocks, slopes):
    bsz, t_new, _ = qa.shape
    page_size = pool_k.shape[2]
    n_pages = page_table.shape[1]
    past_len = n_pages * page_size
    n_tiles = t_new * MOBA_TOPK * (MOBA_BLOCK // page_size)
    new_spec = pl.BlockSpec((None, t_new, A_HEAD_DIM), lambda b, h, pt, blk: (b, 0, h))
    hbm_spec = pl.BlockSpec(memory_space=pl.ANY)
    return pl.pallas_call(
        functools.partial(_moba_paged_kernel, layer=layer, t_new=t_new, page_size=page_size,
                          past_len=past_len),
        grid_spec=pltpu.PrefetchScalarGridSpec(
            num_scalar_prefetch=2,
            grid=(bsz, A_HEADS),
            in_specs=[pl.BlockSpec((None, 1, LANES), lambda b, h, pt, blk: (h, 0, 0)),
                      new_spec, new_spec, new_spec, hbm_spec, hbm_spec],
            out_specs=new_spec,
            scratch_shapes=[pltpu.VMEM((2, n_tiles, page_size, A_HEAD_DIM), F32),
                            pltpu.VMEM((2, n_tiles, page_size, A_HEAD_DIM), F32),
                            pltpu.SemaphoreType.DMA((2,))]),
        out_shape=jax.ShapeDtypeStruct((bsz, t_new, A_WIDTH), F32),
        compiler_params=_params("arbitrary", "arbitrary"),
        name="moba_paged",
    )(page_table, blocks, slopes, qa, ka, va, pool_k, pool_v)


def _ret_epilogue(o, g, nw):
    var = jnp.mean(o * o, axis=-1, keepdims=True)
    return (o * lax.rsqrt(var + EPS) * nw * _silu(g)).astype(BF16)


def _ret_chunk_kernel(q_ref, k_ref, v_ref, g_ref, din_ref, qd_ref, kd_ref, cd_ref, s0_ref, nw_ref,
                      o_ref, s_ref):
    @pl.when(pl.program_id(2) == 0)
    def _():
        s_ref[...] = s0_ref[...]

    k_t = (k_ref[...] * kd_ref[...]).T
    for hh in range(2):
        qb = q_ref[:, hh * R_QK_DIM:(hh + 1) * R_QK_DIM].astype(BF16)
        kb = k_ref[:, hh * R_QK_DIM:(hh + 1) * R_QK_DIM].astype(BF16)
        vb = v_ref[:, hh * R_V_DIM:(hh + 1) * R_V_DIM].astype(BF16)
        s = s_ref[hh]
        inner = lax.dot_general(qb, kb, NT_DIMS, preferred_element_type=F32) * din_ref[hh]
        o = (jnp.dot(inner.astype(BF16), vb, preferred_element_type=F32)
             + jnp.dot(qb, s.astype(BF16), preferred_element_type=F32) * qd_ref[hh])
        kdt = k_t[hh * R_QK_DIM:(hh + 1) * R_QK_DIM, :].astype(BF16)
        s_ref[hh] = s * cd_ref[hh] + jnp.dot(kdt, vb, preferred_element_type=F32)
        o_ref[:, hh * R_V_DIM:(hh + 1) * R_V_DIM] = _ret_epilogue(
            o, g_ref[:, hh * R_V_DIM:(hh + 1) * R_V_DIM], nw_ref[...])


def _ret_tables(chunk):
    log_gamma = jnp.log1p(-jnp.exp2(-5.0 - jnp.arange(R_HEADS, dtype=F32)))
    idx = jnp.arange(chunk, dtype=F32)
    diff = idx[:, None] - idx[None, :]
    causal = diff >= 0
    decay_in = jnp.where(causal[None], jnp.exp(jnp.where(causal, diff, 0.0)[None] * log_gamma[:, None, None]), 0.0)
    q_dec = jnp.exp((idx + 1)[None, :] * log_gamma[:, None])
    k_dec = jnp.exp((chunk - 1 - idx)[None, :] * log_gamma[:, None])
    chunk_dec = jnp.exp(chunk * log_gamma)
    return decay_in, q_dec, k_dec, chunk_dec


def _pair_lanes(k_dec):
    h, c = k_dec.shape
    t = jnp.broadcast_to(k_dec.reshape(h // 2, 2, c, 1), (h // 2, 2, c, R_QK_DIM))
    return t.transpose(0, 2, 1, 3).reshape(h // 2, c, 2 * R_QK_DIM)


def _retention_prompt(r, s0, ret_norm_w, batch, seq):
    c = math.gcd(seq, R_CHUNK)
    nc = seq // c
    decay_in, q_dec, k_dec, chunk_dec = _ret_tables(c)
    qd = q_dec[:, :, None]
    kd = _pair_lanes(k_dec)
    cd = jnp.broadcast_to(chunk_dec[:, None, None], (R_HEADS, 1, LANES))
    qk_blocks = R_QK_WIDTH // LANES
    v_off = 2 * R_QK_WIDTH // (2 * R_V_DIM)
    g_off = (2 * R_QK_WIDTH + R_WIDTH) // (2 * R_V_DIM)
    return pl.pallas_call(
        _ret_chunk_kernel,
        grid=(batch, R_HEADS // 2, nc),
        in_specs=[pl.BlockSpec((c, LANES), lambda b, p, i: (b * nc + i, p)),
                  pl.BlockSpec((c, LANES), lambda b, p, i: (b * nc + i, qk_blocks + p)),
                  pl.BlockSpec((c, 2 * R_V_DIM), lambda b, p, i: (b * nc + i, v_off + p)),
                  pl.BlockSpec((c, 2 * R_V_DIM), lambda b, p, i: (b * nc + i, g_off + p)),
                  pl.BlockSpec((2, c, c), lambda b, p, i: (p, 0, 0)),
                  pl.BlockSpec((2, c, 1), lambda b, p, i: (p, 0, 0)),
                  pl.BlockSpec((None, c, LANES), lambda b, p, i: (p, 0, 0)),
                  pl.BlockSpec((2, 1, LANES), lambda b, p, i: (p, 0, 0)),
                  pl.BlockSpec((None, 2, R_QK_DIM, R_V_DIM), lambda b, p, i: (b, p, 0, 0)),
                  pl.BlockSpec((1, R_V_DIM), lambda b, p, i: (0, 0))],
        out_specs=[pl.BlockSpec((c, 2 * R_V_DIM), lambda b, p, i: (b * nc + i, p)),
                   pl.BlockSpec((None, 2, R_QK_DIM, R_V_DIM), lambda b, p, i: (b, p, 0, 0))],
        out_shape=[jax.ShapeDtypeStruct((batch * seq, R_WIDTH), BF16),
                   jax.ShapeDtypeStruct((batch, R_HEADS, R_QK_DIM, R_V_DIM), F32)],
        compiler_params=_params("parallel", "parallel", "arbitrary"),
        name="retention_chunks",
    )(r, r, r, r, decay_in, qd, kd, cd, s0, ret_norm_w)


def _ret_step_kernel(q_ref, k_ref, v_ref, g_ref, din_ref, qd_ref, kd_ref, cd_ref, s0_ref, nw_ref,
                     o_ref, s_ref, *, t_new):
    rows = q_ref.shape[0]
    bsz = rows // t_new
    k_t = (k_ref[...] * kd_ref[...]).T
    seq_of_row = lax.broadcasted_iota(I32, (rows, bsz * R_QK_DIM), 0) // t_new
    seq_of_col = lax.broadcasted_iota(I32, (rows, bsz * R_QK_DIM), 1) // R_QK_DIM
    own_q = seq_of_row == seq_of_col
    seq_of_srow = lax.broadcasted_iota(I32, (bsz * R_QK_DIM, rows), 0) // R_QK_DIM
    seq_of_scol = lax.broadcasted_iota(I32, (bsz * R_QK_DIM, rows), 1) // t_new
    own_k = seq_of_srow == seq_of_scol
    for hh in range(2):
        q = q_ref[:, hh * R_QK_DIM:(hh + 1) * R_QK_DIM]
        qb = q.astype(BF16)
        kb = k_ref[:, hh * R_QK_DIM:(hh + 1) * R_QK_DIM].astype(BF16)
        vb = v_ref[:, hh * R_V_DIM:(hh + 1) * R_V_DIM].astype(BF16)
        s = s0_ref[:, hh].reshape(bsz * R_QK_DIM, R_V_DIM)
        inner = lax.dot_general(qb, kb, NT_DIMS, preferred_element_type=F32) * din_ref[hh]
        q_wide = jnp.where(own_q, jnp.concatenate([q] * bsz, axis=1), 0.0).astype(BF16)
        o = (jnp.dot(inner.astype(BF16), vb, preferred_element_type=F32)
             + jnp.dot(q_wide, s.astype(BF16), preferred_element_type=F32) * qd_ref[hh])
        kdt = k_t[hh * R_QK_DIM:(hh + 1) * R_QK_DIM, :]
        k_tall = jnp.where(own_k, jnp.concatenate([kdt] * bsz, axis=0), 0.0).astype(BF16)
        s_new = s * cd_ref[hh] + jnp.dot(k_tall, vb, preferred_element_type=F32)
        s_ref[:, hh] = s_new.reshape(bsz, R_QK_DIM, R_V_DIM)
        o_ref[:, hh * R_V_DIM:(hh + 1) * R_V_DIM] = _ret_epilogue(
            o, g_ref[:, hh * R_V_DIM:(hh + 1) * R_V_DIM], nw_ref[...])


def _retention_step(r, s0, ret_norm_w, batch, t_new):
    rows = batch * t_new
    decay_in, q_dec, k_dec, chunk_dec = _ret_tables(t_new)
    same_seq = jnp.eye(batch, dtype=F32)
    din = jnp.einsum("hij,ab->haibj", decay_in, same_seq).reshape(R_HEADS, rows, rows)
    qd = jnp.tile(q_dec, (1, batch))[:, :, None]
    kd = _pair_lanes(jnp.tile(k_dec, (1, batch)))
    cd = jnp.broadcast_to(chunk_dec[:, None, None], (R_HEADS, 1, LANES))
    qk_blocks = R_QK_WIDTH // LANES
    v_off = 2 * R_QK_WIDTH // (2 * R_V_DIM)
    g_off = (2 * R_QK_WIDTH + R_WIDTH) // (2 * R_V_DIM)
    return pl.pallas_call(
        functools.partial(_ret_step_kernel, t_new=t_new),
        grid=(R_HEADS // 2,),
        in_specs=[pl.BlockSpec((rows, LANES), lambda p: (0, p)),
                  pl.BlockSpec((rows, LANES), lambda p: (0, qk_blocks + p)),
                  pl.BlockSpec((rows, 2 * R_V_DIM), lambda p: (0, v_off + p)),
                  pl.BlockSpec((rows, 2 * R_V_DIM), lambda p: (0, g_off + p)),
                  pl.BlockSpec((2, rows, rows), lambda p: (p, 0, 0)),
                  pl.BlockSpec((2, rows, 1), lambda p: (p, 0, 0)),
                  pl.BlockSpec((None, rows, LANES), lambda p: (p, 0, 0)),
                  pl.BlockSpec((2, 1, LANES), lambda p: (p, 0, 0)),
                  pl.BlockSpec((batch, 2, R_QK_DIM, R_V_DIM), lambda p: (0, p, 0, 0)),
                  pl.BlockSpec((1, R_V_DIM), lambda p: (0, 0))],
        out_specs=[pl.BlockSpec((rows, 2 * R_V_DIM), lambda p: (0, p)),
                   pl.BlockSpec((batch, 2, R_QK_DIM, R_V_DIM), lambda p: (0, p, 0, 0))],
        out_shape=[jax.ShapeDtypeStruct((rows, R_WIDTH), BF16),
                   jax.ShapeDtypeStruct((batch, R_HEADS, R_QK_DIM, R_V_DIM), F32)],
        compiler_params=_params("parallel"),
        name="retention_step",
    )(r, r, r, r, din, qd, kd, cd, s0, ret_norm_w)


def _outproj_kernel(a_ref, r_ref, x_ref, g1_ref, sh_ref, sc_ref, nw_ref, wa_ref, wr_ref,
                    x1_ref, h2_ref):
    mixed = (jnp.dot(a_ref[...].astype(BF16), wa_ref[...], preferred_element_type=F32)
             + jnp.dot(r_ref[...].astype(BF16), wr_ref[...], preferred_element_type=F32))
    x1 = x_ref[...] + g1_ref[...] * mixed
    x1_ref[...] = x1
    var = jnp.mean(x1 * x1, axis=-1, keepdims=True)
    h2 = x1 * lax.rsqrt(var + EPS) * nw_ref[...]
    h2_ref[...] = (h2 * (1 + sc_ref[...]) + sh_ref[...]).astype(BF16)


def _outproj(out_a, out_r, x, gate, shift, scale, norm_w, w_out, tm, tiles_per_mod):
    m, d = x.shape
    mod_rows = shift.shape[1]
    mod_spec = pl.BlockSpec((None, mod_rows, d), lambda i: (i // tiles_per_mod, 0, 0))
    return pl.pallas_call(
        _outproj_kernel,
        grid=(m // tm,),
        in_specs=[pl.BlockSpec((tm, A_WIDTH), lambda i: (i, 0)),
                  pl.BlockSpec((tm, R_WIDTH), lambda i: (i, 0)),
                  pl.BlockSpec((tm, d), lambda i: (i, 0)),
                  mod_spec, mod_spec, mod_spec,
                  pl.BlockSpec((1, d), lambda i: (0, 0)),
                  pl.BlockSpec((A_WIDTH, d), lambda i: (0, 0)),
                  pl.BlockSpec((R_WIDTH, d), lambda i: (A_WIDTH // R_WIDTH, 0))],
        out_specs=[pl.BlockSpec((tm, d), lambda i: (i, 0)),
                   pl.BlockSpec((tm, d), lambda i: (i, 0))],
        out_shape=[jax.ShapeDtypeStruct((m, d), F32), jax.ShapeDtypeStruct((m, d), BF16)],
        compiler_params=_params("parallel"),
        name="outproj",
    )(out_a, out_r, x, gate, shift, scale, norm_w, w_out, w_out)


def _pruned_pairs():
    return [(a, P_TOPK // (a + 1)) for a in range(P_TOPK)]


N_CAND = sum(nb for _, nb in _pruned_pairs())
CAND_ROWS = -(-N_CAND // 8) * 8


def _top_rows(x, k, val_ref, idx_ref):
    nrows = x.shape[0]
    row = lax.broadcasted_iota(I32, x.shape, 0).astype(F32)
    for i in range(k):
        mx = jnp.max(x, axis=0, keepdims=True)
        am = jnp.min(jnp.where(x == mx, row, float(nrows)), axis=0, keepdims=True)
        val_ref[i:i + 1, :] = mx
        idx_ref[i:i + 1, :] = am
        x = jnp.where(row == am, -jnp.inf, x)


def _peer_route_kernel(h_ref, wq_ref, sk_ref, i1_ref, i2_ref, g_ref,
                       s1_ref, k1_ref, s2_ref, k2_ref, cand_ref, c1_ref, c2_ref):
    tm = h_ref.shape[0]
    qp = jnp.dot(h_ref[...], wq_ref[...], preferred_element_type=F32)
    cand_ref[...] = jnp.full(cand_ref.shape, -jnp.inf, F32)
    c1_ref[...] = jnp.zeros(c1_ref.shape, F32)
    c2_ref[...] = jnp.zeros(c2_ref.shape, F32)
    row = lax.broadcasted_iota(I32, (CAND_ROWS, tm), 0).astype(F32)
    for hd in range(P_HEADS):
        for half, (s_ref, k_ref) in enumerate(((s1_ref, k1_ref), (s2_ref, k2_ref))):
            col = (hd * 2 + half) * LANES
            q_part = qp[:, col:col + LANES].astype(BF16)
            scores = lax.dot_general(sk_ref[hd * 2 + half], q_part, NT_DIMS,
                                     preferred_element_type=F32)
            _top_rows(scores, P_TOPK, s_ref, k_ref)
        off = 0
        for a, nb in _pruned_pairs():
            cand_ref[off:off + nb, :] = s1_ref[a:a + 1, :] + s2_ref[0:nb, :]
            c1_ref[off:off + nb, :] = jnp.broadcast_to(k1_ref[a:a + 1, :], (nb, tm))
            c2_ref[off:off + nb, :] = k2_ref[0:nb, :]
            off += nb
        cand = cand_ref[...]
        c1 = c1_ref[...]
        c2 = c2_ref[...]
        best, e1, e2 = [], [], []
        for _ in range(P_TOPK):
            mx = jnp.max(cand, axis=0, keepdims=True)
            am = jnp.min(jnp.where(cand == mx, row, float(CAND_ROWS)), axis=0, keepdims=True)
            hit = row == am
            best.append(mx)
            e1.append(jnp.max(jnp.where(hit, c1, -1.0), axis=0, keepdims=True))
            e2.append(jnp.max(jnp.where(hit, c2, -1.0), axis=0, keepdims=True))
            cand = jnp.where(hit, -jnp.inf, cand)
        top = best[0]
        exps = [jnp.exp(v - top) for v in best]
        denom = exps[0]
        for v in exps[1:]:
            denom = denom + v
        for i in range(P_TOPK):
            slot = hd * P_TOPK + i
            i1_ref[slot:slot + 1, :] = e1[i].astype(I32)
            i2_ref[slot:slot + 1, :] = e2[i].astype(I32)
            g_ref[slot:slot + 1, :] = exps[i] / denom


def _peer_route(h2, w_query, sub_keys, tm):
    m, d = h2.shape
    qd = w_query.shape[1]
    slot_spec = pl.BlockSpec((P_SLOTS, tm), lambda i: (0, i))
    return pl.pallas_call(
        _peer_route_kernel,
        grid=(m // tm,),
        in_specs=[pl.BlockSpec((tm, d), lambda i: (i, 0)),
                  pl.BlockSpec((d, qd), lambda i: (0, 0)),
                  pl.BlockSpec(sub_keys.shape, lambda i: (0, 0, 0))],
        out_specs=[slot_spec, slot_spec, slot_spec],
        out_shape=[jax.ShapeDtypeStruct((P_SLOTS, m), I32),
                   jax.ShapeDtypeStruct((P_SLOTS, m), I32),
                   jax.ShapeDtypeStruct((P_SLOTS, m), F32)],
        scratch_shapes=[pltpu.VMEM((P_TOPK, tm), F32)] * 4 + [pltpu.VMEM((CAND_ROWS, tm), F32)] * 3,
        compiler_params=_params("parallel"),
        name="peer_route",
    )(h2, w_query, sub_keys)


SUBLANES = 8


def _peer_mask_kernel(i1_ref, i2_ref, g_ref, o_ref):
    key = lax.broadcasted_iota(I32, (P_NKEYS, P_SLOTS), 0)

    def body(grp, carry):
        base = pl.multiple_of(grp * SUBLANES, SUBLANES)
        mats = []
        for u in range(SUBLANES):
            i1 = i1_ref[pl.ds(base + u, 1), :]
            i2 = i2_ref[pl.ds(base + u, 1), :]
            g = g_ref[pl.ds(base + u, 1), :]
            g_hi = g.astype(BF16).astype(F32)
            g_lo = g - g_hi
            hit1 = key == i1
            a = jnp.concatenate([jnp.where(hit1, g_hi, 0.0).astype(BF16),
                                 jnp.where(hit1, g_lo, 0.0).astype(BF16)], axis=1)
            onehot2 = jnp.where(key == i2, 1.0, 0.0).astype(BF16)
            bm = jnp.concatenate([onehot2, onehot2], axis=1)
            mats.append(lax.dot_general(a, bm, NT_DIMS, preferred_element_type=F32))
        o_ref[grp] = jnp.swapaxes(jnp.stack(mats, axis=0), 0, 1)
        return carry

    lax.fori_loop(0, o_ref.shape[0], body, 0)


def _peer_mask(i1, i2, gates, tt):
    m = i1.shape[0]
    slot_spec = pl.BlockSpec((tt, P_SLOTS), lambda i: (i, 0))
    return pl.pallas_call(
        _peer_mask_kernel,
        grid=(m // tt,),
        in_specs=[slot_spec, slot_spec, slot_spec],
        out_specs=pl.BlockSpec((tt // SUBLANES, P_NKEYS, SUBLANES, P_NKEYS), lambda i: (i, 0, 0, 0)),
        out_shape=jax.ShapeDtypeStruct((m // SUBLANES, P_NKEYS, SUBLANES, P_NKEYS), F32),
        compiler_params=_params("parallel"),
        name="peer_mask",
    )(i1, i2, gates)


def _peer_mlp_kernel(h_ref, dn_ref, up_ref, m_ref, x_ref, g2_ref, o_ref, acc_ref):
    e = pl.program_id(1)

    @pl.when(e == 0)
    def _():
        acc_ref[...] = jnp.zeros(acc_ref.shape, F32)

    pre = lax.dot_general(h_ref[...], dn_ref[...], NT_DIMS, preferred_element_type=F32)
    act = 0.5 * pre * (1.0 + lax.erf(pre * (0.5 ** 0.5)))
    tm = act.shape[0]
    parts = []
    for a in range(m_ref.shape[1]):
        gate = m_ref[:, a].reshape(tm, P_NKEYS)
        parts.append((act[:, a * P_NKEYS:(a + 1) * P_NKEYS] * gate).astype(BF16))
    w = jnp.concatenate(parts, axis=1)
    acc_ref[...] += jnp.dot(w, up_ref[...], preferred_element_type=F32)

    @pl.when(e == pl.num_programs(1) - 1)
    def _():
        o_ref[...] = x_ref[...] + g2_ref[...] * acc_ref[...]


def _peer_mlp(h2, down, up, mask, x1, gate, tm, te, tiles_per_mod):
    m, d = h2.shape
    n_exp = down.shape[0]
    mod_rows = gate.shape[1]
    assert te % P_NKEYS == 0 and tm % SUBLANES == 0
    return pl.pallas_call(
        _peer_mlp_kernel,
        grid=(m // tm, n_exp // te),
        in_specs=[pl.BlockSpec((tm, d), lambda i, e: (i, 0)),
                  pl.BlockSpec((te, d), lambda i, e: (e, 0)),
                  pl.BlockSpec((te, d), lambda i, e: (e, 0)),
                  pl.BlockSpec((tm // SUBLANES, te // P_NKEYS, SUBLANES, P_NKEYS),
                               lambda i, e: (i, e, 0, 0)),
                  pl.BlockSpec((tm, d), lambda i, e: (i, 0)),
                  pl.BlockSpec((None, mod_rows, d), lambda i, e: (i // tiles_per_mod, 0, 0))],
        out_specs=pl.BlockSpec((tm, d), lambda i, e: (i, 0)),
        out_shape=jax.ShapeDtypeStruct((m, d), F32),
        scratch_shapes=[pltpu.VMEM((tm, d), F32)],
        compiler_params=_params("parallel", "arbitrary"),
        name="peer_mlp",
    )(h2, down, up, mask, x1, gate)


def _alibi_slope_rows():
    slopes = jnp.exp2(-8.0 * jnp.arange(1, A_HEADS + 1, dtype=F32) / A_HEADS)
    return jnp.broadcast_to(slopes[:, None, None], (A_HEADS, 1, LANES))


def _layer(x, mod, pool_k, pool_v, layer, page_table, s0, w, *, tm, tm_mlp):
    bsz, seq, d = x.shape
    m = bsz * seq
    xf = x.reshape(m, d)
    chunks = jnp.split(mod, 6, axis=-1)
    if seq % tm == 0:
        mods = [c[:, None, :] for c in chunks]
        tiles_per_mod = seq // tm
        tiles_per_mod_mlp = seq // tm_mlp
    else:
        assert m == tm == tm_mlp
        mods = [jnp.repeat(c, seq, axis=0)[None] for c in chunks]
        tiles_per_mod = tiles_per_mod_mlp = 1
    sh1, sc1, g1, sh2, sc2, g2 = mods

    proj = functools.partial(_inproj, xf, w["norm1_w"], sh1, sc1, w["w_in"], tm=tm,
                             tiles_per_mod=tiles_per_mod)
    qa = proj(col0=0, ncols=A_WIDTH, aux=w["q_norm_w"], mode="headnorm")
    ka = proj(col0=A_WIDTH, ncols=A_WIDTH, aux=w["k_norm_w"], mode="headnorm")
    va = proj(col0=2 * A_WIDTH, ncols=A_WIDTH, aux=w["ones_a"], mode="scale")
    r = proj(col0=3 * A_WIDTH, ncols=2 * R_QK_WIDTH + 2 * R_WIDTH, aux=w["r_scale"], mode="scale")

    slopes = _alibi_slope_rows()
    if pool_k is None:
        out_a = _moba_prompt(qa, ka, va, slopes, bsz, seq)
        out_r, s_new = _retention_prompt(r, s0, w["ret_norm_w"], bsz, seq)
    else:
        kmean = _page_means(pool_k, layer, page_table)
        q3 = qa.reshape(bsz, seq, A_WIDTH)
        q_rep = jnp.broadcast_to(q3[:, None], (bsz, A_HEADS, seq, A_WIDTH)).reshape(bsz, A_HEADS * seq, A_WIDTH)
        choice = _block_choice(q_rep, kmean.reshape(bsz, kmean.shape[1], A_WIDTH))
        blocks = choice[:, :, :MOBA_TOPK].reshape(-1)
        out_a = _moba_paged(q3, ka.reshape(bsz, seq, A_WIDTH), va.reshape(bsz, seq, A_WIDTH),
                            pool_k, pool_v, layer, page_table, blocks, slopes).reshape(m, A_WIDTH)
        out_r, s_new = _retention_step(r, s0, w["ret_norm_w"], bsz, seq)

    tm_small = min(tm, 256)
    x1, h2 = _outproj(out_a, out_r, xf, g1, sh2, sc2, w["norm2_w"], w["w_out"], tm_small,
                      tiles_per_mod * (tm // tm_small))

    i1, i2, gates = _peer_route(h2, w["peer_w_query"], w["peer_sub_keys"], tm=tm_small)
    mask = _peer_mask(i1.T, i2.T, gates.T, tt=min(m, 64))
    y = _peer_mlp(h2, w["peer_down"], w["peer_up"], mask, x1, g2,
                  tm=tm_mlp, te=512, tiles_per_mod=tiles_per_mod_mlp)
    k_out = ka.reshape(bsz, seq, A_HEADS, A_HEAD_DIM)
    v_out = va.reshape(bsz, seq, A_HEADS, A_HEAD_DIM)
    return y.reshape(bsz, seq, d), k_out, v_out, s_new


def kernel(x_prompt, x_sample, cache_k, cache_v, state_ret, page_table, c_prompt, c_sample, w_ada, b_ada, norm1_w, w_in, q_norm_w, k_norm_w, ret_norm_w, w_out, norm2_w, peer_w_query, peer_sub_keys, peer_down, peer_up):
    depth = w_ada.shape[0]
    n_prompt = c_prompt.shape[0]
    n_sample = c_sample.shape[0]
    pad = (-(n_prompt + n_sample)) % 8
    d = x_prompt.shape[-1]
    hp, hs = x_prompt, x_sample
    s0_prompt = jnp.zeros((n_prompt, R_HEADS, R_QK_DIM, R_V_DIM), F32)
    r_scale = jnp.concatenate([jnp.ones((1, R_QK_WIDTH), F32),
                               jnp.full((1, R_QK_WIDTH), R_QK_DIM ** -0.5, F32),
                               jnp.ones((1, 2 * R_WIDTH), F32)], axis=1)
    outs = [[] for _ in range(6)]
    for l in range(depth):
        c_all = jnp.concatenate([c_prompt, c_sample, jnp.zeros((pad, d), F32)], axis=0)
        mod = _ada(c_all, w_ada[l], b_ada[l][None])
        w = dict(norm1_w=norm1_w[l][None], w_in=w_in[l].astype(BF16), q_norm_w=q_norm_w[l][None],
                 k_norm_w=k_norm_w[l][None], ret_norm_w=ret_norm_w[l][None],
                 w_out=w_out[l].astype(BF16), norm2_w=norm2_w[l][None],
                 peer_w_query=peer_w_query[l].astype(BF16),
                 peer_sub_keys=peer_sub_keys[l].reshape(2 * P_HEADS, P_NKEYS, -1).astype(BF16),
                 peer_down=peer_down[l].astype(BF16), peer_up=peer_up[l].astype(BF16),
                 ones_a=jnp.ones((1, A_WIDTH), F32), r_scale=r_scale)
        hp, k1, v1, s1 = _layer(hp, mod[:n_prompt], None, None, l, None, s0_prompt, w, tm=512, tm_mlp=512)
        hs, k2, v2, s2 = _layer(hs, mod[n_prompt:n_prompt + n_sample], cache_k, cache_v, l, page_table,
                                state_ret[l], w, tm=hs.shape[0] * hs.shape[1], tm_mlp=hs.shape[0] * hs.shape[1])
        for lst, val in zip(outs, (k1, v1, s1, k2, v2, s2)):
            lst.append(val)
    return (hp, hs) + tuple(jnp.stack(o) for o in outs)
```

```python
import functools
import math

import jax
import jax.numpy as jnp
from jax import lax
from jax.experimental import pallas as pl
from jax.experimental.pallas import tpu as pltpu

F32 = jnp.float32
BF16 = jnp.bfloat16
I32 = jnp.int32

EPS = 1e-6
NEG = -1e30

A_HEADS = 8
A_HEAD_DIM = 128
A_WIDTH = A_HEADS * A_HEAD_DIM
MOBA_BLOCK = 256
MOBA_TOPK = 3
R_HEADS = 8
R_QK_DIM = 64
R_V_DIM = 128
R_QK_WIDTH = R_HEADS * R_QK_DIM
R_WIDTH = R_HEADS * R_V_DIM
RET_CHUNK = 256
P_HEADS = 8
P_NKEYS = 128
P_TOPK = 16
P_SLOTS = P_HEADS * P_TOPK

LANES = 128
VMEM_LIMIT_BYTES = 48 * 1024 * 1024

NT_DIMS = (((1,), (1,)), ((), ()))


def _params(*semantics):
    return pltpu.CompilerParams(dimension_semantics=semantics, vmem_limit_bytes=VMEM_LIMIT_BYTES)


def _silu(x):
    return x * jax.nn.sigmoid(x)


def _ada_kernel(c_ref, w_ref, b_ref, o_ref):
    s = _silu(c_ref[...]).astype(BF16)
    o_ref[...] = jnp.dot(s, w_ref[...].astype(BF16), preferred_element_type=F32) + b_ref[...]


def _ada(c, w, b):
    rows, d = c.shape
    n = w.shape[1]
    tn = 1024
    return pl.pallas_call(
        _ada_kernel,
        grid=(n // tn,),
        in_specs=[pl.BlockSpec((rows, d), lambda j: (0, 0)),
                  pl.BlockSpec((d, tn), lambda j: (0, j)),
                  pl.BlockSpec((1, tn), lambda j: (0, j))],
        out_specs=pl.BlockSpec((rows, tn), lambda j: (0, j)),
        out_shape=jax.ShapeDtypeStruct((rows, n), F32),
        compiler_params=_params("parallel"),
        name="ada",
    )(c, w, b)


def _inproj_kernel(x_ref, nw_ref, sh_ref, sc_ref, w_ref, aux_ref, o_ref, *, mode):
    x = x_ref[...]
    var = jnp.mean(x * x, axis=-1, keepdims=True)
    h = x * lax.rsqrt(var + EPS) * nw_ref[...]
    h = h * (1 + sc_ref[...]) + sh_ref[...]
    y = jnp.dot(h.astype(BF16), w_ref[...], preferred_element_type=F32)
    if mode == "headnorm":
        for hd in range(y.shape[1] // A_HEAD_DIM):
            yh = y[:, hd * A_HEAD_DIM:(hd + 1) * A_HEAD_DIM]
            v = jnp.mean(yh * yh, axis=-1, keepdims=True)
            o_ref[:, hd * A_HEAD_DIM:(hd + 1) * A_HEAD_DIM] = yh * lax.rsqrt(v + EPS) * aux_ref[...]
    elif mode == "scale":
        o_ref[...] = y * aux_ref[...]
    else:
        o_ref[...] = y


def _inproj(x, norm_w, shift, scale, w, col0, ncols, aux, mode, tm, tiles_per_mod):
    m, d = x.shape
    tn = 1024
    assert ncols % tn == 0 and col0 % tn == 0 and m % tm == 0
    mod_rows = shift.shape[1]
    mod_spec = pl.BlockSpec((None, mod_rows, d), lambda i, j: (i // tiles_per_mod, 0, 0))
    if mode == "headnorm":
        aux_spec = pl.BlockSpec((1, A_HEAD_DIM), lambda i, j: (0, 0))
    else:
        aux_spec = pl.BlockSpec((1, tn), lambda i, j: (0, j))
    return pl.pallas_call(
        functools.partial(_inproj_kernel, mode=mode),
        grid=(m // tm, ncols // tn),
        in_specs=[pl.BlockSpec((tm, d), lambda i, j: (i, 0)),
                  pl.BlockSpec((1, d), lambda i, j: (0, 0)),
                  mod_spec, mod_spec,
                  pl.BlockSpec((d, tn), lambda i, j: (0, col0 // tn + j)),
                  aux_spec],
        out_specs=pl.BlockSpec((tm, tn), lambda i, j: (i, j)),
        out_shape=jax.ShapeDtypeStruct((m, ncols), F32),
        compiler_params=_params("parallel", "arbitrary"),
        name="inproj_" + mode,
    )(x, norm_w, shift, scale, w, aux)


def _moba_prompt_kernel(slope_ref, q_ref, k_ref, v_ref, o_ref, km_ref, vt_ref, *, seq):
    qi = pl.program_id(2)
    blk = MOBA_BLOCK
    nb = seq // blk

    @pl.when(qi == 0)
    def _():
        km_ref[...] = jnp.mean(k_ref[...].reshape(nb, blk, A_HEAD_DIM), axis=1)
        vt_ref[...] = v_ref[...].T.astype(BF16)

    q_t = q_ref[...].T
    gate = jnp.dot(km_ref[...], q_t, precision=lax.Precision.HIGHEST,
                   preferred_element_type=F32)
    g = [jnp.where(n < qi, gate[n:n + 1, :], NEG) for n in range(nb)]
    scale = A_HEAD_DIM ** -0.5
    slope = slope_ref[...][:, :1]
    col_terms = []
    for n in range(nb):
        cnt = jnp.zeros((1, blk), F32)
        for m in range(nb):
            if m < n:
                cnt = cnt + jnp.where(g[m] >= g[n], 1.0, 0.0)
            elif m > n:
                cnt = cnt + jnp.where(g[m] > g[n], 1.0, 0.0)
        chosen = jnp.where(n < qi, jnp.where(cnt < MOBA_TOPK, 1.0, 0.0), jnp.where(n == qi, 1.0, 0.0))
        ahead = ((qi - n) * blk).astype(F32)
        col_terms.append(jnp.where(chosen > 0.5, -slope * ahead, NEG))

    rc = (lax.broadcasted_iota(I32, (blk, blk), 1) - lax.broadcasted_iota(I32, (blk, blk), 0))
    bias_past = -slope * rc.astype(F32)
    bias_own = jnp.where(rc >= 0, bias_past, NEG)
    q_tb = q_t.astype(BF16)

    def attend(n_keys_blocks):
        pieces = []
        for n in range(n_keys_blocks):
            kb = k_ref[n * blk:(n + 1) * blk, :].astype(BF16)
            s_t = jnp.dot(kb, q_tb, preferred_element_type=F32)
            pieces.append(s_t * scale + jnp.where(n == qi, bias_own, bias_past) + col_terms[n])
        logits = jnp.concatenate(pieces, axis=0)
        mx = jnp.max(logits, axis=0, keepdims=True)
        p = jnp.exp(logits - mx)
        denom = jnp.sum(p, axis=0, keepdims=True)
        out_t = jnp.dot(vt_ref[:, :n_keys_blocks * blk], p.astype(BF16), preferred_element_type=F32)
        o_ref[...] = (out_t / denom).T.astype(o_ref.dtype)

    half = nb // 2
    if half == 0:
        attend(nb)
    else:
        @pl.when(qi < half)
        def _():
            attend(half)

        @pl.when(qi >= half)
        def _():
            attend(nb)


def _moba_prompt(qa, ka, va, slopes, batch, seq):
    blk = MOBA_BLOCK
    nq = seq // blk
    return pl.pallas_call(
        functools.partial(_moba_prompt_kernel, seq=seq),
        grid=(batch, A_HEADS, nq),
        in_specs=[pl.BlockSpec((None, 1, LANES), lambda b, h, i: (h, 0, 0)),
                  pl.BlockSpec((blk, A_HEAD_DIM), lambda b, h, i: (b * nq + i, h)),
                  pl.BlockSpec((seq, A_HEAD_DIM), lambda b, h, i: (b, h)),
                  pl.BlockSpec((seq, A_HEAD_DIM), lambda b, h, i: (b, h))],
        out_specs=pl.BlockSpec((blk, A_HEAD_DIM), lambda b, h, i: (b * nq + i, h)),
        out_shape=jax.ShapeDtypeStruct((batch * seq, A_WIDTH), BF16),
        scratch_shapes=[pltpu.VMEM((nq, A_HEAD_DIM), F32),
                        pltpu.VMEM((A_HEAD_DIM, seq), BF16)],
        compiler_params=_params("parallel", "parallel", "arbitrary"),
        name="moba_prompt",
    )(slopes, qa, ka, va)


MAX_PAGES_PER_STEP = 16


def _page_mean_kernel(pt_ref, *refs):
    del pt_ref
    o_ref = refs[-1]
    pages_per_block = (len(refs) - 1) // o_ref.shape[0]
    for j in range(o_ref.shape[0]):
        acc = None
        rows = 0
        for u in range(pages_per_block):
            page = refs[j * pages_per_block + u]
            part = jnp.sum(page[...], axis=0)
            rows += page.shape[0]
            acc = part if acc is None else acc + part
        o_ref[j] = acc * (1.0 / rows)


def _page_means(pool, layer, page_table):
    page_size = pool.shape[2]
    bsz, n_pages = page_table.shape
    pages_per_block = MOBA_BLOCK // page_size
    pages_per_step = math.gcd(n_pages, MAX_PAGES_PER_STEP)
    assert pages_per_step % pages_per_block == 0
    blocks_per_step = pages_per_step // pages_per_block
    n_blk = n_pages // pages_per_block

    def page_spec(u):
        return pl.BlockSpec((None, None, page_size, A_HEADS, A_HEAD_DIM),
                            lambda b, g, pt: (layer, pt[b, g * pages_per_step + u], 0, 0, 0))

    return pl.pallas_call(
        _page_mean_kernel,
        grid_spec=pltpu.PrefetchScalarGridSpec(
            num_scalar_prefetch=1,
            grid=(bsz, n_pages // pages_per_step),
            in_specs=[page_spec(u) for u in range(pages_per_step)],
            out_specs=pl.BlockSpec((None, blocks_per_step, A_HEADS, A_HEAD_DIM),
                                   lambda b, g, pt: (b, g, 0, 0))),
        out_shape=jax.ShapeDtypeStruct((bsz, n_blk, A_HEADS, A_HEAD_DIM), F32),
        compiler_params=_params("parallel", "arbitrary"),
        name="page_means",
    )(page_table, *([pool] * pages_per_step))


def _block_choice_kernel(q_ref, km_ref, o_ref):
    rows = q_ref.shape[0]
    t_new = rows // A_HEADS
    n_blk = km_ref.shape[0]
    q = q_ref[...]
    r_head = lax.broadcasted_iota(I32, q.shape, 0) // t_new
    c_head = lax.broadcasted_iota(I32, q.shape, 1) // A_HEAD_DIM
    qbd = jnp.where(r_head == c_head, q, 0.0)
    gate = lax.dot_general(qbd, km_ref[...], NT_DIMS, precision=lax.Precision.HIGHEST,
                           preferred_element_type=F32)
    lane = lax.broadcasted_iota(I32, gate.shape, 1).astype(F32)
    out_lane = lax.broadcasted_iota(I32, (rows, LANES), 1)
    res = jnp.zeros((rows, LANES), F32)
    for it in range(MOBA_TOPK):
        mx = jnp.max(gate, axis=1, keepdims=True)
        idx = jnp.min(jnp.where(gate == mx, lane, float(n_blk)), axis=1, keepdims=True)
        res = jnp.where(out_lane == it, idx, res)
        gate = jnp.where(lane == idx, -jnp.inf, gate)
    o_ref[...] = res.astype(I32)


def _block_choice(q_rep, kmean):
    bsz, rows, _ = q_rep.shape
    n_blk = kmean.shape[1]
    return pl.pallas_call(
        _block_choice_kernel,
        grid=(bsz,),
        in_specs=[pl.BlockSpec((None, rows, A_WIDTH), lambda b: (b, 0, 0)),
                  pl.BlockSpec((None, n_blk, A_WIDTH), lambda b: (b, 0, 0))],
        out_specs=pl.BlockSpec((None, rows, LANES), lambda b: (b, 0, 0)),
        out_shape=jax.ShapeDtypeStruct((bsz, rows, LANES), I32),
        compiler_params=_params("parallel"),
        name="block_choice",
    )(q_rep, kmean)


def _moba_paged_kernel(pt_ref, blk_ref, slope_ref, q_ref, kn_ref, vn_ref, pk_ref, pv_ref, o_ref,
                       kbuf, vbuf, sem, *, layer, t_new, page_size, past_len):
    pages_per_block = MOBA_BLOCK // page_size
    tiles_per_q = MOBA_TOPK * pages_per_block
    n_heads = pl.num_programs(1)
    step = pl.program_id(0) * n_heads + pl.program_id(1)
    n_steps = pl.num_programs(0) * n_heads
    slot = step % 2

    def chosen_block(step_i, t, j):
        return blk_ref[(step_i * t_new + t) * MOBA_TOPK + j]

    def tile_copies(step_i, slot_i):
        b = step_i // n_heads
        h = step_i % n_heads
        copies = []
        for t in range(t_new):
            for j in range(MOBA_TOPK):
                block = chosen_block(step_i, t, j)
                for half in range(pages_per_block):
                    page = pt_ref[b, block * pages_per_block + half]
                    idx = t * tiles_per_q + j * pages_per_block + half
                    copies.append(pltpu.make_async_copy(pk_ref.at[layer, page, :, h, :],
                                                        kbuf.at[slot_i, idx], sem.at[slot_i]))
                    copies.append(pltpu.make_async_copy(pv_ref.at[layer, page, :, h, :],
                                                        vbuf.at[slot_i, idx], sem.at[slot_i]))
        return copies

    @pl.when(step == 0)
    def _():
        for c in tile_copies(step, slot):
            c.start()

    @pl.when(step + 1 < n_steps)
    def _():
        for c in tile_copies(step + 1, 1 - slot):
            c.start()

    for c in tile_copies(step, slot):
        c.wait()

    scale = A_HEAD_DIM ** -0.5
    slope = slope_ref[...][:, :1]
    row = lax.broadcasted_iota(I32, (page_size, 1), 0)
    row_new = lax.broadcasted_iota(I32, (t_new, 1), 0)
    kn = kn_ref[...]
    vn = vn_ref[...]
    for t in range(t_new):
        qt = q_ref[t:t + 1, :]
        cols = []
        for j in range(MOBA_TOPK):
            block = chosen_block(step, t, j)
            for half in range(pages_per_block):
                kt = kbuf[slot, t * tiles_per_q + j * pages_per_block + half]
                lg = jnp.sum(kt * qt, axis=-1, keepdims=True) * scale
                dist = (past_len + t) - (block * MOBA_BLOCK + half * page_size + row)
                cols.append(lg - slope * dist.astype(F32))
        dist_new = t - row_new
        lg_new = jnp.sum(kn * qt, axis=-1, keepdims=True) * scale - slope * dist_new.astype(F32)
        lg_new = jnp.where(dist_new >= 0, lg_new, NEG)
        mx = jnp.max(lg_new, axis=0, keepdims=True)
        for c in cols:
            mx = jnp.maximum(mx, jnp.max(c, axis=0, keepdims=True))
        p_new = jnp.exp(lg_new - mx)
        denom = jnp.sum(p_new, axis=0, keepdims=True)
        acc = jnp.sum(p_new * vn, axis=0, keepdims=True)
        for idx, c in enumerate(cols):
            p = jnp.exp(c - mx)
            denom = denom + jnp.sum(p, axis=0, keepdims=True)
            acc = acc + jnp.sum(p * vbuf[slot, t * tiles_per_q + idx], axis=0, keepdims=True)
        o_ref[t:t + 1, :] = acc / denom


def _moba_paged(qa, ka, va, pool_k, pool_v, layer, page_table, blocks, slopes):
    bsz, t_new, _ = qa.shape
    page_size = pool_k.shape[2]
    n_pages = page_table.shape[1]
    past_len = n_pages * page_size
    n_tiles = t_new * MOBA_TOPK * (MOBA_BLOCK // page_size)
    new_spec = pl.BlockSpec((None, t_new, A_HEAD_DIM), lambda b, h, pt, blk: (b, 0, h))
    hbm_spec = pl.BlockSpec(memory_space=pl.ANY)
    return pl.pallas_call(
        functools.partial(_moba_paged_kernel, layer=layer, t_new=t_new, page_size=page_size,
                          past_len=past_len),
        grid_spec=pltpu.PrefetchScalarGridSpec(
            num_scalar_prefetch=2,
            grid=(bsz, A_HEADS),
            in_specs=[pl.BlockSpec((None, 1, LANES), lambda b, h, pt, blk: (h, 0, 0)),
                      new_spec, new_spec, new_spec, hbm_spec, hbm_spec],
            out_specs=new_spec,
            scratch_shapes=[pltpu.VMEM((2, n_tiles, page_size, A_HEAD_DIM), F32),
                            pltpu.VMEM((2, n_tiles, page_size, A_HEAD_DIM), F32),
                            pltpu.SemaphoreType.DMA((2,))]),
        out_shape=jax.ShapeDtypeStruct((bsz, t_new, A_WIDTH), F32),
        compiler_params=_params("arbitrary", "arbitrary"),
        name="moba_paged",
    )(page_table, blocks, slopes, qa, ka, va, pool_k, pool_v)


def _ret_epilogue(o, g, nw):
    var = jnp.mean(o * o, axis=-1, keepdims=True)
    return (o * lax.rsqrt(var + EPS) * nw * _silu(g)).astype(BF16)


def _ret_chunk_kernel(q_ref, k_ref, v_ref, g_ref, din_ref, qd_ref, kd_ref, cd_ref, s0_ref, nw_ref,
                      o_ref, s_ref):
    @pl.when(pl.program_id(2) == 0)
    def _():
        s_ref[...] = s0_ref[...]

    k_t = (k_ref[...] * kd_ref[...]).T
    for hh in range(2):
        qb = q_ref[:, hh * R_QK_DIM:(hh + 1) * R_QK_DIM].astype(BF16)
        kb = k_ref[:, hh * R_QK_DIM:(hh + 1) * R_QK_DIM].astype(BF16)
        vb = v_ref[:, hh * R_V_DIM:(hh + 1) * R_V_DIM].astype(BF16)
        s = s_ref[hh]
        inner = lax.dot_general(qb, kb, NT_DIMS, preferred_element_type=F32) * din_ref[hh]
        o = (jnp.dot(inner.astype(BF16), vb, preferred_element_type=F32)
             + jnp.dot(qb, s.astype(BF16), preferred_element_type=F32) * qd_ref[hh])
        kdt = k_t[hh * R_QK_DIM:(hh + 1) * R_QK_DIM, :].astype(BF16)
        s_ref[hh] = s * cd_ref[hh] + jnp.dot(kdt, vb, preferred_element_type=F32)
        o_ref[:, hh * R_V_DIM:(hh + 1) * R_V_DIM] = _ret_epilogue(
            o, g_ref[:, hh * R_V_DIM:(hh + 1) * R_V_DIM], nw_ref[...])


def _ret_tables(chunk):
    log_gamma = jnp.log1p(-jnp.exp2(-5.0 - jnp.arange(R_HEADS, dtype=F32)))
    idx = jnp.arange(chunk, dtype=F32)
    diff = idx[:, None] - idx[None, :]
    causal = diff >= 0
    decay_in = jnp.where(causal[None], jnp.exp(jnp.where(causal, diff, 0.0)[None] * log_gamma[:, None, None]), 0.0)
    q_dec = jnp.exp((idx + 1)[None, :] * log_gamma[:, None])
    k_dec = jnp.exp((chunk - 1 - idx)[None, :] * log_gamma[:, None])
    chunk_dec = jnp.exp(chunk * log_gamma)
    return decay_in, q_dec, k_dec, chunk_dec


def _pair_lanes(k_dec):
    h, c = k_dec.shape
    t = jnp.broadcast_to(k_dec.reshape(h // 2, 2, c, 1), (h // 2, 2, c, R_QK_DIM))
    return t.transpose(0, 2, 1, 3).reshape(h // 2, c, 2 * R_QK_DIM)


def _retention_prompt(r, s0, ret_norm_w, batch, seq):
    c = math.gcd(seq, RET_CHUNK)
    nc = seq // c
    decay_in, q_dec, k_dec, chunk_dec = _ret_tables(c)
    qd = q_dec[:, :, None]
    kd = _pair_lanes(k_dec)
    cd = jnp.broadcast_to(chunk_dec[:, None, None], (R_HEADS, 1, LANES))
    qk_blocks = R_QK_WIDTH // LANES
    v_off = 2 * R_QK_WIDTH // (2 * R_V_DIM)
    g_off = (2 * R_QK_WIDTH + R_WIDTH) // (2 * R_V_DIM)
    return pl.pallas_call(
        _ret_chunk_kernel,
        grid=(batch, R_HEADS // 2, nc),
        in_specs=[pl.BlockSpec((c, LANES), lambda b, p, i: (b * nc + i, p)),
                  pl.BlockSpec((c, LANES), lambda b, p, i: (b * nc + i, qk_blocks + p)),
                  pl.BlockSpec((c, 2 * R_V_DIM), lambda b, p, i: (b * nc + i, v_off + p)),
                  pl.BlockSpec((c, 2 * R_V_DIM), lambda b, p, i: (b * nc + i, g_off + p)),
                  pl.BlockSpec((2, c, c), lambda b, p, i: (p, 0, 0)),
                  pl.BlockSpec((2, c, 1), lambda b, p, i: (p, 0, 0)),
                  pl.BlockSpec((None, c, LANES), lambda b, p, i: (p, 0, 0)),
                  pl.BlockSpec((2, 1, LANES), lambda b, p, i: (p, 0, 0)),
                  pl.BlockSpec((None, 2, R_QK_DIM, R_V_DIM), lambda b, p, i: (b, p, 0, 0)),
                  pl.BlockSpec((1, R_V_DIM), lambda b, p, i: (0, 0))],
        out_specs=[pl.BlockSpec((c, 2 * R_V_DIM), lambda b, p, i: (b * nc + i, p)),
                   pl.BlockSpec((None, 2, R_QK_DIM, R_V_DIM), lambda b, p, i: (b, p, 0, 0))],
        out_shape=[jax.ShapeDtypeStruct((batch * seq, R_WIDTH), BF16),
                   jax.ShapeDtypeStruct((batch, R_HEADS, R_QK_DIM, R_V_DIM), F32)],
        compiler_params=_params("parallel", "parallel", "arbitrary"),
        name="retention_chunks",
    )(r, r, r, r, decay_in, qd, kd, cd, s0, ret_norm_w)


def _ret_step_kernel(q_ref, k_ref, v_ref, g_ref, din_ref, qd_ref, kd_ref, cd_ref, s0_ref, nw_ref,
                     o_ref, s_ref, *, t_new):
    rows = q_ref.shape[0]
    bsz = rows // t_new
    k_t = (k_ref[...] * kd_ref[...]).T
    seq_of_row = lax.broadcasted_iota(I32, (rows, bsz * R_QK_DIM), 0) // t_new
    seq_of_col = lax.broadcasted_iota(I32, (rows, bsz * R_QK_DIM), 1) // R_QK_DIM
    own_q = seq_of_row == seq_of_col
    seq_of_srow = lax.broadcasted_iota(I32, (bsz * R_QK_DIM, rows), 0) // R_QK_DIM
    seq_of_scol = lax.broadcasted_iota(I32, (bsz * R_QK_DIM, rows), 1) // t_new
    own_k = seq_of_srow == seq_of_scol
    for hh in range(2):
        q = q_ref[:, hh * R_QK_DIM:(hh + 1) * R_QK_DIM]
        qb = q.astype(BF16)
        kb = k_ref[:, hh * R_QK_DIM:(hh + 1) * R_QK_DIM].astype(BF16)
        vb = v_ref[:, hh * R_V_DIM:(hh + 1) * R_V_DIM].astype(BF16)
        s = s0_ref[:, hh].reshape(bsz * R_QK_DIM, R_V_DIM)
        inner = lax.dot_general(qb, kb, NT_DIMS, preferred_element_type=F32) * din_ref[hh]
        q_wide = jnp.where(own_q, jnp.concatenate([q] * bsz, axis=1), 0.0).astype(BF16)
        o = (jnp.dot(inner.astype(BF16), vb, preferred_element_type=F32)
             + jnp.dot(q_wide, s.astype(BF16), preferred_element_type=F32) * qd_ref[hh])
        kdt = k_t[hh * R_QK_DIM:(hh + 1) * R_QK_DIM, :]
        k_tall = jnp.where(own_k, jnp.concatenate([kdt] * bsz, axis=0), 0.0).astype(BF16)
        s_new = s * cd_ref[hh] + jnp.dot(k_tall, vb, preferred_element_type=F32)
        s_ref[:, hh] = s_new.reshape(bsz, R_QK_DIM, R_V_DIM)
        o_ref[:, hh * R_V_DIM:(hh + 1) * R_V_DIM] = _ret_epilogue(
            o, g_ref[:, hh * R_V_DIM:(hh + 1) * R_V_DIM], nw_ref[...])


def _retention_step(r, s0, ret_norm_w, batch, t_new):
    rows = batch * t_new
    decay_in, q_dec, k_dec, chunk_dec = _ret_tables(t_new)
    same_seq = jnp.eye(batch, dtype=F32)
    din = jnp.einsum("hij,ab->haibj", decay_in, same_seq).reshape(R_HEADS, rows, rows)
    qd = jnp.tile(q_dec, (1, batch))[:, :, None]
    kd = _pair_lanes(jnp.tile(k_dec, (1, batch)))
    cd = jnp.broadcast_to(chunk_dec[:, None, None], (R_HEADS, 1, LANES))
    qk_blocks = R_QK_WIDTH // LANES
    v_off = 2 * R_QK_WIDTH // (2 * R_V_DIM)
    g_off = (2 * R_QK_WIDTH + R_WIDTH) // (2 * R_V_DIM)
    return pl.pallas_call(
        functools.partial(_ret_step_kernel, t_new=t_new),
        grid=(R_HEADS // 2,),
        in_specs=[pl.BlockSpec((rows, LANES), lambda p: (0, p)),
                  pl.BlockSpec((rows, LANES), lambda p: (0, qk_blocks + p)),
                  pl.BlockSpec((rows, 2 * R_V_DIM), lambda p: (0, v_off + p)),
                  pl.BlockSpec((rows, 2 * R_V_DIM), lambda p: (0, g_off + p)),
                  pl.BlockSpec((2, rows, rows), lambda p: (p, 0, 0)),
                  pl.BlockSpec((2, rows, 1), lambda p: (p, 0, 0)),
                  pl.BlockSpec((None, rows, LANES), lambda p: (p, 0, 0)),
                  pl.BlockSpec((2, 1, LANES), lambda p: (p, 0, 0)),
                  pl.BlockSpec((batch, 2, R_QK_DIM, R_V_DIM), lambda p: (0, p, 0, 0)),
                  pl.BlockSpec((1, R_V_DIM), lambda p: (0, 0))],
        out_specs=[pl.BlockSpec((rows, 2 * R_V_DIM), lambda p: (0, p)),
                   pl.BlockSpec((batch, 2, R_QK_DIM, R_V_DIM), lambda p: (0, p, 0, 0))],
        out_shape=[jax.ShapeDtypeStruct((rows, R_WIDTH), BF16),
                   jax.ShapeDtypeStruct((batch, R_HEADS, R_QK_DIM, R_V_DIM), F32)],
        compiler_params=_params("parallel"),
        name="retention_step",
    )(r, r, r, r, din, qd, kd, cd, s0, ret_norm_w)


def _outproj_kernel(a_ref, r_ref, x_ref, g1_ref, sh_ref, sc_ref, nw_ref, wa_ref, wr_ref,
                    x1_ref, h2_ref):
    mixed = (jnp.dot(a_ref[...].astype(BF16), wa_ref[...], preferred_element_type=F32)
             + jnp.dot(r_ref[...].astype(BF16), wr_ref[...], preferred_element_type=F32))
    x1 = x_ref[...] + g1_ref[...] * mixed
    x1_ref[...] = x1
    var = jnp.mean(x1 * x1, axis=-1, keepdims=True)
    h2 = x1 * lax.rsqrt(var + EPS) * nw_ref[...]
    h2_ref[...] = (h2 * (1 + sc_ref[...]) + sh_ref[...]).astype(BF16)


def _outproj(out_a, out_r, x, gate, shift, scale, norm_w, w_out, tm, tiles_per_mod):
    m, d = x.shape
    mod_rows = shift.shape[1]
    mod_spec = pl.BlockSpec((None, mod_rows, d), lambda i: (i // tiles_per_mod, 0, 0))
    return pl.pallas_call(
        _outproj_kernel,
        grid=(m // tm,),
        in_specs=[pl.BlockSpec((tm, A_WIDTH), lambda i: (i, 0)),
                  pl.BlockSpec((tm, R_WIDTH), lambda i: (i, 0)),
                  pl.BlockSpec((tm, d), lambda i: (i, 0)),
                  mod_spec, mod_spec, mod_spec,
                  pl.BlockSpec((1, d), lambda i: (0, 0)),
                  pl.BlockSpec((A_WIDTH, d), lambda i: (0, 0)),
                  pl.BlockSpec((R_WIDTH, d), lambda i: (A_WIDTH // R_WIDTH, 0))],
        out_specs=[pl.BlockSpec((tm, d), lambda i: (i, 0)),
                   pl.BlockSpec((tm, d), lambda i: (i, 0))],
        out_shape=[jax.ShapeDtypeStruct((m, d), F32), jax.ShapeDtypeStruct((m, d), BF16)],
        compiler_params=_params("parallel"),
        name="outproj",
    )(out_a, out_r, x, gate, shift, scale, norm_w, w_out, w_out)


def _pruned_pairs():
    return [(a, P_TOPK // (a + 1)) for a in range(P_TOPK)]


N_CAND = sum(nb for _, nb in _pruned_pairs())
CAND_ROWS = -(-N_CAND // 8) * 8


def _top_rows(x, k, val_ref, idx_ref):
    nrows = x.shape[0]
    row = lax.broadcasted_iota(I32, x.shape, 0).astype(F32)
    for i in range(k):
        mx = jnp.max(x, axis=0, keepdims=True)
        am = jnp.min(jnp.where(x == mx, row, float(nrows)), axis=0, keepdims=True)
        val_ref[i:i + 1, :] = mx
        idx_ref[i:i + 1, :] = am
        x = jnp.where(row == am, -jnp.inf, x)


def _peer_route_kernel(h_ref, wq_ref, sk_ref, i1_ref, i2_ref, g_ref,
                       s1_ref, k1_ref, s2_ref, k2_ref, cand_ref, c1_ref, c2_ref):
    tm = h_ref.shape[0]
    qp = jnp.dot(h_ref[...], wq_ref[...], preferred_element_type=F32)
    cand_ref[...] = jnp.full(cand_ref.shape, -jnp.inf, F32)
    c1_ref[...] = jnp.zeros(c1_ref.shape, F32)
    c2_ref[...] = jnp.zeros(c2_ref.shape, F32)
    row = lax.broadcasted_iota(I32, (CAND_ROWS, tm), 0).astype(F32)
    for hd in range(P_HEADS):
        for half, (s_ref, k_ref) in enumerate(((s1_ref, k1_ref), (s2_ref, k2_ref))):
            col = (hd * 2 + half) * LANES
            q_part = qp[:, col:col + LANES].astype(BF16)
            scores = lax.dot_general(sk_ref[hd * 2 + half], q_part, NT_DIMS,
                                     preferred_element_type=F32)
            _top_rows(scores, P_TOPK, s_ref, k_ref)
        off = 0
        for a, nb in _pruned_pairs():
            cand_ref[off:off + nb, :] = s1_ref[a:a + 1, :] + s2_ref[0:nb, :]
            c1_ref[off:off + nb, :] = jnp.broadcast_to(k1_ref[a:a + 1, :], (nb, tm))
            c2_ref[off:off + nb, :] = k2_ref[0:nb, :]
            off += nb
        cand = cand_ref[...]
        c1 = c1_ref[...]
        c2 = c2_ref[...]
        best, e1, e2 = [], [], []
        for _ in range(P_TOPK):
            mx = jnp.max(cand, axis=0, keepdims=True)
            am = jnp.min(jnp.where(cand == mx, row, float(CAND_ROWS)), axis=0, keepdims=True)
            hit = row == am
            best.append(mx)
            e1.append(jnp.max(jnp.where(hit, c1, -1.0), axis=0, keepdims=True))
            e2.append(jnp.max(jnp.where(hit, c2, -1.0), axis=0, keepdims=True))
            cand = jnp.where(hit, -jnp.inf, cand)
        top = best[0]
        exps = [jnp.exp(v - top) for v in best]
        denom = exps[0]
        for v in exps[1:]:
            denom = denom + v
        for i in range(P_TOPK):
            slot = hd * P_TOPK + i
            i1_ref[slot:slot + 1, :] = e1[i].astype(I32)
            i2_ref[slot:slot + 1, :] = e2[i].astype(I32)
            g_ref[slot:slot + 1, :] = exps[i] / denom


def _peer_route(h2, w_query, sub_keys, tm):
    m, d = h2.shape
    qd = w_query.shape[1]
    slot_spec = pl.BlockSpec((P_SLOTS, tm), lambda i: (0, i))
    return pl.pallas_call(
        _peer_route_kernel,
        grid=(m // tm,),
        in_specs=[pl.BlockSpec((tm, d), lambda i: (i, 0)),
                  pl.BlockSpec((d, qd), lambda i: (0, 0)),
                  pl.BlockSpec(sub_keys.shape, lambda i: (0, 0, 0))],
        out_specs=[slot_spec, slot_spec, slot_spec],
        out_shape=[jax.ShapeDtypeStruct((P_SLOTS, m), I32),
                   jax.ShapeDtypeStruct((P_SLOTS, m), I32),
                   jax.ShapeDtypeStruct((P_SLOTS, m), F32)],
        scratch_shapes=[pltpu.VMEM((P_TOPK, tm), F32)] * 4 + [pltpu.VMEM((CAND_ROWS, tm), F32)] * 3,
        compiler_params=_params("parallel"),
        name="peer_route",
    )(h2, w_query, sub_keys)


SUBLANES = 8


def _peer_mask_kernel(i1_ref, i2_ref, g_ref, o_ref):
    key = lax.broadcasted_iota(I32, (P_NKEYS, P_SLOTS), 0)

    def body(grp, carry):
        base = pl.multiple_of(grp * SUBLANES, SUBLANES)
        mats = []
        for u in range(SUBLANES):
            i1 = i1_ref[pl.ds(base + u, 1), :]
            i2 = i2_ref[pl.ds(base + u, 1), :]
            g = g_ref[pl.ds(base + u, 1), :]
            g_hi = g.astype(BF16).astype(F32)
            g_lo = g - g_hi
            hit1 = key == i1
            a = jnp.concatenate([jnp.where(hit1, g_hi, 0.0).astype(BF16),
                                 jnp.where(hit1, g_lo, 0.0).astype(BF16)], axis=1)
            onehot2 = jnp.where(key == i2, 1.0, 0.0).astype(BF16)
            bm = jnp.concatenate([onehot2, onehot2], axis=1)
            mats.append(lax.dot_general(a, bm, NT_DIMS, preferred_element_type=F32))
        o_ref[grp] = jnp.swapaxes(jnp.stack(mats, axis=0), 0, 1)
        return carry

    lax.fori_loop(0, o_ref.shape[0], body, 0)


def _peer_mask(i1, i2, gates, tt):
    m = i1.shape[0]
    slot_spec = pl.BlockSpec((tt, P_SLOTS), lambda i: (i, 0))
    return pl.pallas_call(
        _peer_mask_kernel,
        grid=(m // tt,),
        in_specs=[slot_spec, slot_spec, slot_spec],
        out_specs=pl.BlockSpec((tt // SUBLANES, P_NKEYS, SUBLANES, P_NKEYS), lambda i: (i, 0, 0, 0)),
        out_shape=jax.ShapeDtypeStruct((m // SUBLANES, P_NKEYS, SUBLANES, P_NKEYS), F32),
        compiler_params=_params("parallel"),
        name="peer_mask",
    )(i1, i2, gates)


def _peer_mlp_kernel(h_ref, dn_ref, up_ref, m_ref, x_ref, g2_ref, o_ref, acc_ref):
    e = pl.program_id(1)

    @pl.when(e == 0)
    def _():
        acc_ref[...] = jnp.zeros(acc_ref.shape, F32)

    pre = lax.dot_general(h_ref[...], dn_ref[...], NT_DIMS, preferred_element_type=F32)
    act = 0.5 * pre * (1.0 + lax.erf(pre * (0.5 ** 0.5)))
    tm = act.shape[0]
    parts = []
    for a in range(m_ref.shape[1]):
        gate = m_ref[:, a].reshape(tm, P_NKEYS)
        parts.append((act[:, a * P_NKEYS:(a + 1) * P_NKEYS] * gate).astype(BF16))
    w = jnp.concatenate(parts, axis=1)
    acc_ref[...] += jnp.dot(w, up_ref[...], preferred_element_type=F32)

    @pl.when(e == pl.num_programs(1) - 1)
    def _():
        o_ref[...] = x_ref[...] + g2_ref[...] * acc_ref[...]


def _peer_mlp(h2, down, up, mask, x1, gate, tm, te, tiles_per_mod):
    m, d = h2.shape
    n_exp = down.shape[0]
    mod_rows = gate.shape[1]
    assert te % P_NKEYS == 0 and tm % SUBLANES == 0
    return pl.pallas_call(
        _peer_mlp_kernel,
        grid=(m // tm, n_exp // te),
        in_specs=[pl.BlockSpec((tm, d), lambda i, e: (i, 0), pipeline_mode=pl.Buffered(1)),
                  pl.BlockSpec((te, d), lambda i, e: (e, 0)),
                  pl.BlockSpec((te, d), lambda i, e: (e, 0)),
                  pl.BlockSpec((tm // SUBLANES, te // P_NKEYS, SUBLANES, P_NKEYS),
                               lambda i, e: (i, e, 0, 0)),
                  pl.BlockSpec((tm, d), lambda i, e: (i, 0), pipeline_mode=pl.Buffered(1)),
                  pl.BlockSpec((None, mod_rows, d), lambda i, e: (i // tiles_per_mod, 0, 0))],
        out_specs=pl.BlockSpec((tm, d), lambda i, e: (i, 0)),
        out_shape=jax.ShapeDtypeStruct((m, d), F32),
        scratch_shapes=[pltpu.VMEM((tm, d), F32)],
        compiler_params=_params("parallel", "arbitrary"),
        name="peer_mlp",
    )(h2, down, up, mask, x1, gate)


def _alibi_slope_rows():
    slopes = jnp.exp2(-8.0 * jnp.arange(1, A_HEADS + 1, dtype=F32) / A_HEADS)
    return jnp.broadcast_to(slopes[:, None, None], (A_HEADS, 1, LANES))


def _layer(x, mod, pool_k, pool_v, layer, page_table, s0, w, *, tm, tm_mlp):
    bsz, seq, d = x.shape
    m = bsz * seq
    xf = x.reshape(m, d)
    chunks = jnp.split(mod, 6, axis=-1)
    if seq % tm == 0:
        mods = [c[:, None, :] for c in chunks]
        tiles_per_mod = seq // tm
        tiles_per_mod_mlp = seq // tm_mlp
    else:
        assert m == tm == tm_mlp
        mods = [jnp.repeat(c, seq, axis=0)[None] for c in chunks]
        tiles_per_mod = tiles_per_mod_mlp = 1
    sh1, sc1, g1, sh2, sc2, g2 = mods

    proj = functools.partial(_inproj, xf, w["norm1_w"], sh1, sc1, w["w_in"], tm=tm,
                             tiles_per_mod=tiles_per_mod)
    qa = proj(col0=0, ncols=A_WIDTH, aux=w["q_norm_w"], mode="headnorm")
    ka = proj(col0=A_WIDTH, ncols=A_WIDTH, aux=w["k_norm_w"], mode="headnorm")
    va = proj(col0=2 * A_WIDTH, ncols=A_WIDTH, aux=w["ones_a"], mode="scale")
    r = proj(col0=3 * A_WIDTH, ncols=2 * R_QK_WIDTH + 2 * R_WIDTH, aux=w["r_scale"], mode="scale")

    slopes = _alibi_slope_rows()
    if pool_k is None:
        out_a = _moba_prompt(qa, ka, va, slopes, bsz, seq)
        out_r, s_new = _retention_prompt(r, s0, w["ret_norm_w"], bsz, seq)
    else:
        kmean = _page_means(pool_k, layer, page_table)
        q3 = qa.reshape(bsz, seq, A_WIDTH)
        q_rep = jnp.broadcast_to(q3[:, None], (bsz, A_HEADS, seq, A_WIDTH)).reshape(bsz, A_HEADS * seq, A_WIDTH)
        choice = _block_choice(q_rep, kmean.reshape(bsz, kmean.shape[1], A_WIDTH))
        blocks = choice[:, :, :MOBA_TOPK].reshape(-1)
        out_a = _moba_paged(q3, ka.reshape(bsz, seq, A_WIDTH), va.reshape(bsz, seq, A_WIDTH),
                            pool_k, pool_v, layer, page_table, blocks, slopes).reshape(m, A_WIDTH)
        out_r, s_new = _retention_step(r, s0, w["ret_norm_w"], bsz, seq)

    tm_small = min(tm, 256)
    x1, h2 = _outproj(out_a, out_r, xf, g1, sh2, sc2, w["norm2_w"], w["w_out"], tm_small,
                      tiles_per_mod * (tm // tm_small))

    i1, i2, gates = _peer_route(h2, w["peer_w_query"], w["peer_sub_keys"], tm=tm_small)
    mask = _peer_mask(i1.T, i2.T, gates.T, tt=min(m, 64))
    y = _peer_mlp(h2, w["peer_down"], w["peer_up"], mask, x1, g2,
                  tm=tm_mlp, te=1024, tiles_per_mod=tiles_per_mod_mlp)
    k_out = ka.reshape(bsz, seq, A_HEADS, A_HEAD_DIM)
    v_out = va.reshape(bsz, seq, A_HEADS, A_HEAD_DIM)
    return y.reshape(bsz, seq, d), k_out, v_out, s_new


def kernel(x_prompt, x_sample, cache_k, cache_v, state_ret, page_table, c_prompt, c_sample, w_ada, b_ada, norm1_w, w_in, q_norm_w, k_norm_w, ret_norm_w, w_out, norm2_w, peer_w_query, peer_sub_keys, peer_down, peer_up):
    depth = w_ada.shape[0]
    n_prompt = c_prompt.shape[0]
    n_sample = c_sample.shape[0]
    pad = (-(n_prompt + n_sample)) % 8
    d = x_prompt.shape[-1]
    hp, hs = x_prompt, x_sample
    s0_prompt = jnp.zeros((n_prompt, R_HEADS, R_QK_DIM, R_V_DIM), F32)
    r_scale = jnp.concatenate([jnp.ones((1, R_QK_WIDTH), F32),
                               jnp.full((1, R_QK_WIDTH), R_QK_DIM ** -0.5, F32),
                               jnp.ones((1, 2 * R_WIDTH), F32)], axis=1)
    outs = [[] for _ in range(6)]
    for l in range(depth):
        c_all = jnp.concatenate([c_prompt, c_sample, jnp.zeros((pad, d), F32)], axis=0)
        mod = _ada(c_all, w_ada[l], b_ada[l][None])
        w = dict(norm1_w=norm1_w[l][None], w_in=w_in[l].astype(BF16), q_norm_w=q_norm_w[l][None],
                 k_norm_w=k_norm_w[l][None], ret_norm_w=ret_norm_w[l][None],
                 w_out=w_out[l].astype(BF16), norm2_w=norm2_w[l][None],
                 peer_w_query=peer_w_query[l].astype(BF16),
                 peer_sub_keys=peer_sub_keys[l].reshape(2 * P_HEADS, P_NKEYS, -1).astype(BF16),
                 peer_down=peer_down[l].astype(BF16), peer_up=peer_up[l].astype(BF16),
                 ones_a=jnp.ones((1, A_WIDTH), F32), r_scale=r_scale)
        hp, k1, v1, s1 = _layer(hp, mod[:n_prompt], None, None, l, None, s0_prompt, w, tm=512, tm_mlp=512)
        hs, k2, v2, s2 = _layer(hs, mod[n_prompt:n_prompt + n_sample], cache_k, cache_v, l, page_table,
                                state_ret[l], w, tm=hs.shape[0] * hs.shape[1], tm_mlp=hs.shape[0] * hs.shape[1])
        for lst, val in zip(outs, (k1, v1, s1, k2, v2, s2)):
            lst.append(val)
    return (hp, hs) + tuple(jnp.stack(o) for o in outs)
```

```python
import functools
import math

import jax
import jax.numpy as jnp
from jax import lax
from jax.experimental import pallas as pl
from jax.experimental.pallas import tpu as pltpu

F32 = jnp.float32
BF16 = jnp.bfloat16
I32 = jnp.int32

EPS = 1e-6
NEG = -1e30

A_HEADS = 8
A_HEAD_DIM = 128
A_WIDTH = A_HEADS * A_HEAD_DIM
MOBA_BLOCK = 256
MOBA_TOPK = 3
R_HEADS = 8
R_QK_DIM = 64
R_V_DIM = 128
R_QK_WIDTH = R_HEADS * R_QK_DIM
R_WIDTH = R_HEADS * R_V_DIM
RET_CHUNK = 256
P_HEADS = 8
P_NKEYS = 128
P_TOPK = 16
P_SLOTS = P_HEADS * P_TOPK

LANES = 128
VMEM_LIMIT_BYTES = 48 * 1024 * 1024

NT_DIMS = (((1,), (1,)), ((), ()))


def _params(*semantics):
    return pltpu.CompilerParams(dimension_semantics=semantics, vmem_limit_bytes=VMEM_LIMIT_BYTES)


def _silu(x):
    return x * jax.nn.sigmoid(x)


def _ada_kernel(c_ref, w_ref, b_ref, o_ref):
    s = _silu(c_ref[...]).astype(BF16)
    o_ref[...] = jnp.dot(s, w_ref[...].astype(BF16), preferred_element_type=F32) + b_ref[...]


def _ada(c, w, b):
    rows, d = c.shape
    n = w.shape[1]
    tn = 1024
    return pl.pallas_call(
        _ada_kernel,
        grid=(n // tn,),
        in_specs=[pl.BlockSpec((rows, d), lambda j: (0, 0)),
                  pl.BlockSpec((d, tn), lambda j: (0, j)),
                  pl.BlockSpec((1, tn), lambda j: (0, j))],
        out_specs=pl.BlockSpec((rows, tn), lambda j: (0, j)),
        out_shape=jax.ShapeDtypeStruct((rows, n), F32),
        compiler_params=_params("parallel"),
        name="ada",
    )(c, w, b)


def _inproj_kernel(x_ref, nw_ref, sh_ref, sc_ref, w_ref, aux_ref, o_ref, *, mode):
    x = x_ref[...]
    var = jnp.mean(x * x, axis=-1, keepdims=True)
    h = x * lax.rsqrt(var + EPS) * nw_ref[...]
    h = h * (1 + sc_ref[...]) + sh_ref[...]
    y = jnp.dot(h.astype(BF16), w_ref[...], preferred_element_type=F32)
    if mode == "headnorm":
        for hd in range(y.shape[1] // A_HEAD_DIM):
            yh = y[:, hd * A_HEAD_DIM:(hd + 1) * A_HEAD_DIM]
            v = jnp.mean(yh * yh, axis=-1, keepdims=True)
            o_ref[:, hd * A_HEAD_DIM:(hd + 1) * A_HEAD_DIM] = yh * lax.rsqrt(v + EPS) * aux_ref[...]
    elif mode == "scale":
        o_ref[...] = y * aux_ref[...]
    else:
        o_ref[...] = y


def _inproj(x, norm_w, shift, scale, w, col0, ncols, aux, mode, tm, tiles_per_mod):
    m, d = x.shape
    tn = 1024
    assert ncols % tn == 0 and col0 % tn == 0 and m % tm == 0
    mod_rows = shift.shape[1]
    mod_spec = pl.BlockSpec((None, mod_rows, d), lambda i, j: (i // tiles_per_mod, 0, 0))
    if mode == "headnorm":
        aux_spec = pl.BlockSpec((1, A_HEAD_DIM), lambda i, j: (0, 0))
    else:
        aux_spec = pl.BlockSpec((1, tn), lambda i, j: (0, j))
    return pl.pallas_call(
        functools.partial(_inproj_kernel, mode=mode),
        grid=(m // tm, ncols // tn),
        in_specs=[pl.BlockSpec((tm, d), lambda i, j: (i, 0)),
                  pl.BlockSpec((1, d), lambda i, j: (0, 0)),
                  mod_spec, mod_spec,
                  pl.BlockSpec((d, tn), lambda i, j: (0, col0 // tn + j)),
                  aux_spec],
        out_specs=pl.BlockSpec((tm, tn), lambda i, j: (i, j)),
        out_shape=jax.ShapeDtypeStruct((m, ncols), F32),
        compiler_params=_params("parallel", "arbitrary"),
        name="inproj_" + mode,
    )(x, norm_w, shift, scale, w, aux)


def _block_means(page_refs, o_ref):
    pages_per_block = len(page_refs) // o_ref.shape[0]
    for j in range(o_ref.shape[0]):
        acc = None
        rows = 0
        for u in range(pages_per_block):
            page = page_refs[j * pages_per_block + u]
            part = jnp.sum(page[...], axis=0)
            rows += page.shape[0]
            acc = part if acc is None else acc + part
        o_ref[j] = acc * (1.0 / rows)


def _moba_prompt_kernel(pt_ref, slope_ref, q_ref, k_ref, v_ref, *refs, seq, n_side_pages):
    del pt_ref
    if n_side_pages:
        o_ref, side_ref, km_ref, vt_ref = refs[n_side_pages:]
        _block_means(refs[:n_side_pages], side_ref)
    else:
        o_ref, km_ref, vt_ref = refs
    qi = pl.program_id(2)
    blk = MOBA_BLOCK
    nb = seq // blk

    @pl.when(qi == 0)
    def _():
        km_ref[...] = jnp.mean(k_ref[...].reshape(nb, blk, A_HEAD_DIM), axis=1)
        vt_ref[...] = v_ref[...].T.astype(BF16)

    q_t = q_ref[...].T
    gate = jnp.dot(km_ref[...], q_t, precision=lax.Precision.HIGHEST,
                   preferred_element_type=F32)
    g = [jnp.where(n < qi, gate[n:n + 1, :], NEG) for n in range(nb)]
    scale = A_HEAD_DIM ** -0.5
    slope = slope_ref[...][:, :1]
    col_terms = []
    for n in range(nb):
        cnt = jnp.zeros((1, blk), F32)
        for m in range(nb):
            if m < n:
                cnt = cnt + jnp.where(g[m] >= g[n], 1.0, 0.0)
            elif m > n:
                cnt = cnt + jnp.where(g[m] > g[n], 1.0, 0.0)
        chosen = jnp.where(n < qi, jnp.where(cnt < MOBA_TOPK, 1.0, 0.0), jnp.where(n == qi, 1.0, 0.0))
        ahead = ((qi - n) * blk).astype(F32)
        col_terms.append(jnp.where(chosen > 0.5, -slope * ahead, NEG))

    rc = (lax.broadcasted_iota(I32, (blk, blk), 1) - lax.broadcasted_iota(I32, (blk, blk), 0))
    bias_past = -slope * rc.astype(F32)
    bias_own = jnp.where(rc >= 0, bias_past, NEG)
    q_tb = q_t.astype(BF16)

    def attend(n_keys_blocks):
        pieces = []
        for n in range(n_keys_blocks):
            kb = k_ref[n * blk:(n + 1) * blk, :].astype(BF16)
            s_t = jnp.dot(kb, q_tb, preferred_element_type=F32)
            pieces.append(s_t * scale + jnp.where(n == qi, bias_own, bias_past) + col_terms[n])
        logits = jnp.concatenate(pieces, axis=0)
        mx = jnp.max(logits, axis=0, keepdims=True)
        p = jnp.exp(logits - mx)
        denom = jnp.sum(p, axis=0, keepdims=True)
        out_t = jnp.dot(vt_ref[:, :n_keys_blocks * blk], p.astype(BF16), preferred_element_type=F32)
        o_ref[...] = (out_t / denom).T.astype(o_ref.dtype)

    half = nb // 2
    if half == 0:
        attend(nb)
    else:
        @pl.when(qi < half)
        def _():
            attend(half)

        @pl.when(qi >= half)
        def _():
            attend(nb)


def _side_pages_per_step(side_pool, side_page_table, n_steps):
    if side_pool is None:
        return 0
    bsz, n_pages = side_page_table.shape
    pages_per_block = MOBA_BLOCK // side_pool.shape[2]
    if (bsz * n_pages) % n_steps:
        return 0
    per_step = bsz * n_pages // n_steps
    if per_step % pages_per_block or n_pages % per_step or per_step > MAX_PAGES_PER_STEP:
        return 0
    return per_step


def _moba_prompt(qa, ka, va, slopes, batch, seq, side_pool=None, layer=0, side_page_table=None):
    blk = MOBA_BLOCK
    nq = seq // blk
    n_side = _side_pages_per_step(side_pool, side_page_table, batch * A_HEADS * nq)
    in_specs = [pl.BlockSpec((None, 1, LANES), lambda b, h, i, pt: (h, 0, 0)),
                pl.BlockSpec((blk, A_HEAD_DIM), lambda b, h, i, pt: (b * nq + i, h)),
                pl.BlockSpec((seq, A_HEAD_DIM), lambda b, h, i, pt: (b, h)),
                pl.BlockSpec((seq, A_HEAD_DIM), lambda b, h, i, pt: (b, h))]
    out_specs = [pl.BlockSpec((blk, A_HEAD_DIM), lambda b, h, i, pt: (b * nq + i, h))]
    out_shape = [jax.ShapeDtypeStruct((batch * seq, A_WIDTH), BF16)]
    operands = [slopes, qa, ka, va]
    if n_side:
        page_size = side_pool.shape[2]
        side_bsz, n_pages = side_page_table.shape
        steps_per_seq = n_pages // n_side
        blocks_per_step = n_side // (MOBA_BLOCK // page_size)

        def flat(b, h, i):
            return (b * A_HEADS + h) * nq + i

        def page_spec(u):
            def index(b, h, i, pt):
                s = flat(b, h, i)
                return (layer, pt[s // steps_per_seq, (s % steps_per_seq) * n_side + u], 0, 0, 0)
            return pl.BlockSpec((None, None, page_size, A_HEADS, A_HEAD_DIM), index)

        in_specs += [page_spec(u) for u in range(n_side)]
        operands += [side_pool] * n_side
        out_specs.append(pl.BlockSpec(
            (None, blocks_per_step, A_HEADS, A_HEAD_DIM),
            lambda b, h, i, pt: (flat(b, h, i) // steps_per_seq, flat(b, h, i) % steps_per_seq, 0, 0)))
        out_shape.append(jax.ShapeDtypeStruct(
            (side_bsz, n_pages // (MOBA_BLOCK // page_size), A_HEADS, A_HEAD_DIM), F32))
        prefetch = side_page_table
    else:
        prefetch = jnp.zeros((1, 1), I32)
    outs = pl.pallas_call(
        functools.partial(_moba_prompt_kernel, seq=seq, n_side_pages=n_side),
        grid_spec=pltpu.PrefetchScalarGridSpec(
            num_scalar_prefetch=1,
            grid=(batch, A_HEADS, nq),
            in_specs=in_specs,
            out_specs=out_specs,
            scratch_shapes=[pltpu.VMEM((nq, A_HEAD_DIM), F32),
                            pltpu.VMEM((A_HEAD_DIM, seq), BF16)]),
        out_shape=out_shape,
        compiler_params=_params("parallel", "parallel", "arbitrary"),
        name="moba_prompt",
    )(prefetch, *operands)
    return outs[0], (outs[1] if n_side else None)


MAX_PAGES_PER_STEP = 16


def _page_mean_kernel(pt_ref, *refs):
    del pt_ref
    _block_means(refs[:-1], refs[-1])


def _page_means(pool, layer, page_table):
    page_size = pool.shape[2]
    bsz, n_pages = page_table.shape
    pages_per_block = MOBA_BLOCK // page_size
    pages_per_step = math.gcd(n_pages, MAX_PAGES_PER_STEP)
    assert pages_per_step % pages_per_block == 0
    blocks_per_step = pages_per_step // pages_per_block
    n_blk = n_pages // pages_per_block

    def page_spec(u):
        return pl.BlockSpec((None, None, page_size, A_HEADS, A_HEAD_DIM),
                            lambda b, g, pt: (layer, pt[b, g * pages_per_step + u], 0, 0, 0))

    return pl.pallas_call(
        _page_mean_kernel,
        grid_spec=pltpu.PrefetchScalarGridSpec(
            num_scalar_prefetch=1,
            grid=(bsz, n_pages // pages_per_step),
            in_specs=[page_spec(u) for u in range(pages_per_step)],
            out_specs=pl.BlockSpec((None, blocks_per_step, A_HEADS, A_HEAD_DIM),
                                   lambda b, g, pt: (b, g, 0, 0))),
        out_shape=jax.ShapeDtypeStruct((bsz, n_blk, A_HEADS, A_HEAD_DIM), F32),
        compiler_params=_params("parallel", "arbitrary"),
        name="page_means",
    )(page_table, *([pool] * pages_per_step))


def _block_choice_kernel(q_ref, km_ref, o_ref):
    rows = q_ref.shape[0]
    t_new = rows // A_HEADS
    n_blk = km_ref.shape[0]
    q = q_ref[...]
    r_head = lax.broadcasted_iota(I32, q.shape, 0) // t_new
    c_head = lax.broadcasted_iota(I32, q.shape, 1) // A_HEAD_DIM
    qbd = jnp.where(r_head == c_head, q, 0.0)
    gate = lax.dot_general(qbd, km_ref[...], NT_DIMS, precision=lax.Precision.HIGHEST,
                           preferred_element_type=F32)
    lane = lax.broadcasted_iota(I32, gate.shape, 1).astype(F32)
    out_lane = lax.broadcasted_iota(I32, (rows, LANES), 1)
    res = jnp.zeros((rows, LANES), F32)
    for it in range(MOBA_TOPK):
        mx = jnp.max(gate, axis=1, keepdims=True)
        idx = jnp.min(jnp.where(gate == mx, lane, float(n_blk)), axis=1, keepdims=True)
        res = jnp.where(out_lane == it, idx, res)
        gate = jnp.where(lane == idx, -jnp.inf, gate)
    o_ref[...] = res.astype(I32)


def _block_choice(q_rep, kmean):
    bsz, rows, _ = q_rep.shape
    n_blk = kmean.shape[1]
    return pl.pallas_call(
        _block_choice_kernel,
        grid=(bsz,),
        in_specs=[pl.BlockSpec((None, rows, A_WIDTH), lambda b: (b, 0, 0)),
                  pl.BlockSpec((None, n_blk, A_WIDTH), lambda b: (b, 0, 0))],
        out_specs=pl.BlockSpec((None, rows, LANES), lambda b: (b, 0, 0)),
        out_shape=jax.ShapeDtypeStruct((bsz, rows, LANES), I32),
        compiler_params=_params("parallel"),
        name="block_choice",
    )(q_rep, kmean)


def _moba_paged_kernel(pt_ref, blk_ref, slope_ref, q_ref, kn_ref, vn_ref, pk_ref, pv_ref, o_ref,
                       kbuf, vbuf, sem, *, layer, t_new, page_size, past_len):
    pages_per_block = MOBA_BLOCK // page_size
    tiles_per_q = MOBA_TOPK * pages_per_block
    n_heads = pl.num_programs(1)
    step = pl.program_id(0) * n_heads + pl.program_id(1)
    n_steps = pl.num_programs(0) * n_heads
    slot = step % 2

    def chosen_block(step_i, t, j):
        return blk_ref[(step_i * t_new + t) * MOBA_TOPK + j]

    def tile_copies(step_i, slot_i):
        b = step_i // n_heads
        h = step_i % n_heads
        copies = []
        for t in range(t_new):
            for j in range(MOBA_TOPK):
                block = chosen_block(step_i, t, j)
                for half in range(pages_per_block):
                    page = pt_ref[b, block * pages_per_block + half]
                    idx = t * tiles_per_q + j * pages_per_block + half
                    copies.append(pltpu.make_async_copy(pk_ref.at[layer, page, :, h, :],
                                                        kbuf.at[slot_i, idx], sem.at[slot_i]))
                    copies.append(pltpu.make_async_copy(pv_ref.at[layer, page, :, h, :],
                                                        vbuf.at[slot_i, idx], sem.at[slot_i]))
        return copies

    @pl.when(step == 0)
    def _():
        for c in tile_copies(step, slot):
            c.start()

    @pl.when(step + 1 < n_steps)
    def _():
        for c in tile_copies(step + 1, 1 - slot):
            c.start()

    for c in tile_copies(step, slot):
        c.wait()

    scale = A_HEAD_DIM ** -0.5
    slope = slope_ref[...][:, :1]
    row = lax.broadcasted_iota(I32, (page_size, 1), 0)
    row_new = lax.broadcasted_iota(I32, (t_new, 1), 0)
    kn = kn_ref[...]
    vn = vn_ref[...]
    for t in range(t_new):
        qt = q_ref[t:t + 1, :]
        cols = []
        for j in range(MOBA_TOPK):
            block = chosen_block(step, t, j)
            for half in range(pages_per_block):
                kt = kbuf[slot, t * tiles_per_q + j * pages_per_block + half]
                lg = jnp.sum(kt * qt, axis=-1, keepdims=True) * scale
                dist = (past_len + t) - (block * MOBA_BLOCK + half * page_size + row)
                cols.append(lg - slope * dist.astype(F32))
        dist_new = t - row_new
        lg_new = jnp.sum(kn * qt, axis=-1, keepdims=True) * scale - slope * dist_new.astype(F32)
        lg_new = jnp.where(dist_new >= 0, lg_new, NEG)
        mx = jnp.max(lg_new, axis=0, keepdims=True)
        for c in cols:
            mx = jnp.maximum(mx, jnp.max(c, axis=0, keepdims=True))
        p_new = jnp.exp(lg_new - mx)
        denom = jnp.sum(p_new, axis=0, keepdims=True)
        acc = jnp.sum(p_new * vn, axis=0, keepdims=True)
        for idx, c in enumerate(cols):
            p = jnp.exp(c - mx)
            denom = denom + jnp.sum(p, axis=0, keepdims=True)
            acc = acc + jnp.sum(p * vbuf[slot, t * tiles_per_q + idx], axis=0, keepdims=True)
        o_ref[t:t + 1, :] = acc / denom


def _moba_paged(qa, ka, va, pool_k, pool_v, layer, page_table, blocks, slopes):
    bsz, t_new, _ = qa.shape
    page_size = pool_k.shape[2]
    n_pages = page_table.shape[1]
    past_len = n_pages * page_size
    n_tiles = t_new * MOBA_TOPK * (MOBA_BLOCK // page_size)
    new_spec = pl.BlockSpec((None, t_new, A_HEAD_DIM), lambda b, h, pt, blk: (b, 0, h))
    hbm_spec = pl.BlockSpec(memory_space=pl.ANY)
    return pl.pallas_call(
        functools.partial(_moba_paged_kernel, layer=layer, t_new=t_new, page_size=page_size,
                          past_len=past_len),
        grid_spec=pltpu.PrefetchScalarGridSpec(
            num_scalar_prefetch=2,
            grid=(bsz, A_HEADS),
            in_specs=[pl.BlockSpec((None, 1, LANES), lambda b, h, pt, blk: (h, 0, 0)),
                      new_spec, new_spec, new_spec, hbm_spec, hbm_spec],
            out_specs=new_spec,
            scratch_shapes=[pltpu.VMEM((2, n_tiles, page_size, A_HEAD_DIM), F32),
                            pltpu.VMEM((2, n_tiles, page_size, A_HEAD_DIM), F32),
                            pltpu.SemaphoreType.DMA((2,))]),
        out_shape=jax.ShapeDtypeStruct((bsz, t_new, A_WIDTH), F32),
        compiler_params=_params("arbitrary", "arbitrary"),
        name="moba_paged",
    )(page_table, blocks, slopes, qa, ka, va, pool_k, pool_v)


def _ret_epilogue(o, g, nw):
    var = jnp.mean(o * o, axis=-1, keepdims=True)
    return (o * lax.rsqrt(var + EPS) * nw * _silu(g)).astype(BF16)


def _ret_chunk_kernel(q_ref, k_ref, v_ref, g_ref, din_ref, qd_ref, kd_ref, cd_ref, s0_ref, nw_ref,
                      o_ref, s_ref):
    @pl.when(pl.program_id(2) == 0)
    def _():
        s_ref[...] = s0_ref[...]

    k_t = (k_ref[...] * kd_ref[...]).T
    for hh in range(2):
        qb = q_ref[:, hh * R_QK_DIM:(hh + 1) * R_QK_DIM].astype(BF16)
        kb = k_ref[:, hh * R_QK_DIM:(hh + 1) * R_QK_DIM].astype(BF16)
        vb = v_ref[:, hh * R_V_DIM:(hh + 1) * R_V_DIM].astype(BF16)
        s = s_ref[hh]
        inner = lax.dot_general(qb, kb, NT_DIMS, preferred_element_type=F32) * din_ref[hh]
        o = (jnp.dot(inner.astype(BF16), vb, preferred_element_type=F32)
             + jnp.dot(qb, s.astype(BF16), preferred_element_type=F32) * qd_ref[hh])
        kdt = k_t[hh * R_QK_DIM:(hh + 1) * R_QK_DIM, :].astype(BF16)
        s_ref[hh] = s * cd_ref[hh] + jnp.dot(kdt, vb, preferred_element_type=F32)
        o_ref[:, hh * R_V_DIM:(hh + 1) * R_V_DIM] = _ret_epilogue(
            o, g_ref[:, hh * R_V_DIM:(hh + 1) * R_V_DIM], nw_ref[...])


def _ret_tables(chunk):
    log_gamma = jnp.log1p(-jnp.exp2(-5.0 - jnp.arange(R_HEADS, dtype=F32)))
    idx = jnp.arange(chunk, dtype=F32)
    diff = idx[:, None] - idx[None, :]
    causal = diff >= 0
    decay_in = jnp.where(causal[None], jnp.exp(jnp.where(causal, diff, 0.0)[None] * log_gamma[:, None, None]), 0.0)
    q_dec = jnp.exp((idx + 1)[None, :] * log_gamma[:, None])
    k_dec = jnp.exp((chunk - 1 - idx)[None, :] * log_gamma[:, None])
    chunk_dec = jnp.exp(chunk * log_gamma)
    return decay_in, q_dec, k_dec, chunk_dec


def _pair_lanes(k_dec):
    h, c = k_dec.shape
    t = jnp.broadcast_to(k_dec.reshape(h // 2, 2, c, 1), (h // 2, 2, c, R_QK_DIM))
    return t.transpose(0, 2, 1, 3).reshape(h // 2, c, 2 * R_QK_DIM)


def _retention_prompt(r, s0, ret_norm_w, batch, seq):
    c = math.gcd(seq, RET_CHUNK)
    nc = seq // c
    decay_in, q_dec, k_dec, chunk_dec = _ret_tables(c)
    qd = q_dec[:, :, None]
    kd = _pair_lanes(k_dec)
    cd = jnp.broadcast_to(chunk_dec[:, None, None], (R_HEADS, 1, LANES))
    qk_blocks = R_QK_WIDTH // LANES
    v_off = 2 * R_QK_WIDTH // (2 * R_V_DIM)
    g_off = (2 * R_QK_WIDTH + R_WIDTH) // (2 * R_V_DIM)
    return pl.pallas_call(
        _ret_chunk_kernel,
        grid=(batch, R_HEADS // 2, nc),
        in_specs=[pl.BlockSpec((c, LANES), lambda b, p, i: (b * nc + i, p)),
                  pl.BlockSpec((c, LANES), lambda b, p, i: (b * nc + i, qk_blocks + p)),
                  pl.BlockSpec((c, 2 * R_V_DIM), lambda b, p, i: (b * nc + i, v_off + p)),
                  pl.BlockSpec((c, 2 * R_V_DIM), lambda b, p, i: (b * nc + i, g_off + p)),
                  pl.BlockSpec((2, c, c), lambda b, p, i: (p, 0, 0)),
                  pl.BlockSpec((2, c, 1), lambda b, p, i: (p, 0, 0)),
                  pl.BlockSpec((None, c, LANES), lambda b, p, i: (p, 0, 0)),
                  pl.BlockSpec((2, 1, LANES), lambda b, p, i: (p, 0, 0)),
                  pl.BlockSpec((None, 2, R_QK_DIM, R_V_DIM), lambda b, p, i: (b, p, 0, 0)),
                  pl.BlockSpec((1, R_V_DIM), lambda b, p, i: (0, 0))],
        out_specs=[pl.BlockSpec((c, 2 * R_V_DIM), lambda b, p, i: (b * nc + i, p)),
                   pl.BlockSpec((None, 2, R_QK_DIM, R_V_DIM), lambda b, p, i: (b, p, 0, 0))],
        out_shape=[jax.ShapeDtypeStruct((batch * seq, R_WIDTH), BF16),
                   jax.ShapeDtypeStruct((batch, R_HEADS, R_QK_DIM, R_V_DIM), F32)],
        compiler_params=_params("parallel", "parallel", "arbitrary"),
        name="retention_chunks",
    )(r, r, r, r, decay_in, qd, kd, cd, s0, ret_norm_w)


def _ret_step_kernel(q_ref, k_ref, v_ref, g_ref, din_ref, qd_ref, kd_ref, cd_ref, s0_ref, nw_ref,
                     o_ref, s_ref, *, t_new):
    rows = q_ref.shape[0]
    bsz = rows // t_new
    k_t = (k_ref[...] * kd_ref[...]).T
    seq_of_row = lax.broadcasted_iota(I32, (rows, bsz * R_QK_DIM), 0) // t_new
    seq_of_col = lax.broadcasted_iota(I32, (rows, bsz * R_QK_DIM), 1) // R_QK_DIM
    own_q = seq_of_row == seq_of_col
    seq_of_srow = lax.broadcasted_iota(I32, (bsz * R_QK_DIM, rows), 0) // R_QK_DIM
    seq_of_scol = lax.broadcasted_iota(I32, (bsz * R_QK_DIM, rows), 1) // t_new
    own_k = seq_of_srow == seq_of_scol
    for hh in range(2):
        q = q_ref[:, hh * R_QK_DIM:(hh + 1) * R_QK_DIM]
        qb = q.astype(BF16)
        kb = k_ref[:, hh * R_QK_DIM:(hh + 1) * R_QK_DIM].astype(BF16)
        vb = v_ref[:, hh * R_V_DIM:(hh + 1) * R_V_DIM].astype(BF16)
        s = s0_ref[:, hh].reshape(bsz * R_QK_DIM, R_V_DIM)
        inner = lax.dot_general(qb, kb, NT_DIMS, preferred_element_type=F32) * din_ref[hh]
        q_wide = jnp.where(own_q, jnp.concatenate([q] * bsz, axis=1), 0.0).astype(BF16)
        o = (jnp.dot(inner.astype(BF16), vb, preferred_element_type=F32)
             + jnp.dot(q_wide, s.astype(BF16), preferred_element_type=F32) * qd_ref[hh])
        kdt = k_t[hh * R_QK_DIM:(hh + 1) * R_QK_DIM, :]
        k_tall = jnp.where(own_k, jnp.concatenate([kdt] * bsz, axis=0), 0.0).astype(BF16)
        s_new = s * cd_ref[hh] + jnp.dot(k_tall, vb, preferred_element_type=F32)
        s_ref[:, hh] = s_new.reshape(bsz, R_QK_DIM, R_V_DIM)
        o_ref[:, hh * R_V_DIM:(hh + 1) * R_V_DIM] = _ret_epilogue(
            o, g_ref[:, hh * R_V_DIM:(hh + 1) * R_V_DIM], nw_ref[...])


def _retention_step(r, s0, ret_norm_w, batch, t_new):
    rows = batch * t_new
    decay_in, q_dec, k_dec, chunk_dec = _ret_tables(t_new)
    same_seq = jnp.eye(batch, dtype=F32)
    din = jnp.einsum("hij,ab->haibj", decay_in, same_seq).reshape(R_HEADS, rows, rows)
    qd = jnp.tile(q_dec, (1, batch))[:, :, None]
    kd = _pair_lanes(jnp.tile(k_dec, (1, batch)))
    cd = jnp.broadcast_to(chunk_dec[:, None, None], (R_HEADS, 1, LANES))
    qk_blocks = R_QK_WIDTH // LANES
    v_off = 2 * R_QK_WIDTH // (2 * R_V_DIM)
    g_off = (2 * R_QK_WIDTH + R_WIDTH) // (2 * R_V_DIM)
    return pl.pallas_call(
        functools.partial(_ret_step_kernel, t_new=t_new),
        grid=(R_HEADS // 2,),
        in_specs=[pl.BlockSpec((rows, LANES), lambda p: (0, p)),
                  pl.BlockSpec((rows, LANES), lambda p: (0, qk_blocks + p)),
                  pl.BlockSpec((rows, 2 * R_V_DIM), lambda p: (0, v_off + p)),
                  pl.BlockSpec((rows, 2 * R_V_DIM), lambda p: (0, g_off + p)),
                  pl.BlockSpec((2, rows, rows), lambda p: (p, 0, 0)),
                  pl.BlockSpec((2, rows, 1), lambda p: (p, 0, 0)),
                  pl.BlockSpec((None, rows, LANES), lambda p: (p, 0, 0)),
                  pl.BlockSpec((2, 1, LANES), lambda p: (p, 0, 0)),
                  pl.BlockSpec((batch, 2, R_QK_DIM, R_V_DIM), lambda p: (0, p, 0, 0)),
                  pl.BlockSpec((1, R_V_DIM), lambda p: (0, 0))],
        out_specs=[pl.BlockSpec((rows, 2 * R_V_DIM), lambda p: (0, p)),
                   pl.BlockSpec((batch, 2, R_QK_DIM, R_V_DIM), lambda p: (0, p, 0, 0))],
        out_shape=[jax.ShapeDtypeStruct((rows, R_WIDTH), BF16),
                   jax.ShapeDtypeStruct((batch, R_HEADS, R_QK_DIM, R_V_DIM), F32)],
        compiler_params=_params("parallel"),
        name="retention_step",
    )(r, r, r, r, din, qd, kd, cd, s0, ret_norm_w)


def _outproj_kernel(a_ref, r_ref, x_ref, g1_ref, sh_ref, sc_ref, nw_ref, wa_ref, wr_ref,
                    x1_ref, h2_ref):
    mixed = (jnp.dot(a_ref[...].astype(BF16), wa_ref[...], preferred_element_type=F32)
             + jnp.dot(r_ref[...].astype(BF16), wr_ref[...], preferred_element_type=F32))
    x1 = x_ref[...] + g1_ref[...] * mixed
    x1_ref[...] = x1
    var = jnp.mean(x1 * x1, axis=-1, keepdims=True)
    h2 = x1 * lax.rsqrt(var + EPS) * nw_ref[...]
    h2_ref[...] = (h2 * (1 + sc_ref[...]) + sh_ref[...]).astype(BF16)


def _outproj(out_a, out_r, x, gate, shift, scale, norm_w, w_out, tm, tiles_per_mod):
    m, d = x.shape
    mod_rows = shift.shape[1]
    mod_spec = pl.BlockSpec((None, mod_rows, d), lambda i: (i // tiles_per_mod, 0, 0))
    return pl.pallas_call(
        _outproj_kernel,
        grid=(m // tm,),
        in_specs=[pl.BlockSpec((tm, A_WIDTH), lambda i: (i, 0)),
                  pl.BlockSpec((tm, R_WIDTH), lambda i: (i, 0)),
                  pl.BlockSpec((tm, d), lambda i: (i, 0)),
                  mod_spec, mod_spec, mod_spec,
                  pl.BlockSpec((1, d), lambda i: (0, 0)),
                  pl.BlockSpec((A_WIDTH, d), lambda i: (0, 0)),
                  pl.BlockSpec((R_WIDTH, d), lambda i: (A_WIDTH // R_WIDTH, 0))],
        out_specs=[pl.BlockSpec((tm, d), lambda i: (i, 0)),
                   pl.BlockSpec((tm, d), lambda i: (i, 0))],
        out_shape=[jax.ShapeDtypeStruct((m, d), F32), jax.ShapeDtypeStruct((m, d), BF16)],
        compiler_params=_params("parallel"),
        name="outproj",
    )(out_a, out_r, x, gate, shift, scale, norm_w, w_out, w_out)


def _pruned_pairs():
    return [(a, P_TOPK // (a + 1)) for a in range(P_TOPK)]


N_CAND = sum(nb for _, nb in _pruned_pairs())
CAND_ROWS = -(-N_CAND // 8) * 8


def _top_rows(x, k, val_ref, idx_ref):
    nrows = x.shape[0]
    row = lax.broadcasted_iota(I32, x.shape, 0).astype(F32)
    for i in range(k):
        mx = jnp.max(x, axis=0, keepdims=True)
        am = jnp.min(jnp.where(x == mx, row, float(nrows)), axis=0, keepdims=True)
        val_ref[i:i + 1, :] = mx
        idx_ref[i:i + 1, :] = am
        x = jnp.where(row == am, -jnp.inf, x)


def _peer_route_kernel(h_ref, wq_ref, sk_ref, i1_ref, i2_ref, g_ref,
                       s1_ref, k1_ref, s2_ref, k2_ref, cand_ref, c1_ref, c2_ref):
    tm = h_ref.shape[0]
    qp = jnp.dot(h_ref[...], wq_ref[...], preferred_element_type=F32)
    cand_ref[...] = jnp.full(cand_ref.shape, -jnp.inf, F32)
    c1_ref[...] = jnp.zeros(c1_ref.shape, F32)
    c2_ref[...] = jnp.zeros(c2_ref.shape, F32)
    row = lax.broadcasted_iota(I32, (CAND_ROWS, tm), 0).astype(F32)
    for hd in range(P_HEADS):
        for half, (s_ref, k_ref) in enumerate(((s1_ref, k1_ref), (s2_ref, k2_ref))):
            col = (hd * 2 + half) * LANES
            q_part = qp[:, col:col + LANES].astype(BF16)
            scores = lax.dot_general(sk_ref[hd * 2 + half], q_part, NT_DIMS,
                                     preferred_element_type=F32)
            _top_rows(scores, P_TOPK, s_ref, k_ref)
        off = 0
        for a, nb in _pruned_pairs():
            cand_ref[off:off + nb, :] = s1_ref[a:a + 1, :] + s2_ref[0:nb, :]
            c1_ref[off:off + nb, :] = jnp.broadcast_to(k1_ref[a:a + 1, :], (nb, tm))
            c2_ref[off:off + nb, :] = k2_ref[0:nb, :]
            off += nb
        cand = cand_ref[...]
        c1 = c1_ref[...]
        c2 = c2_ref[...]
        best, e1, e2 = [], [], []
        for _ in range(P_TOPK):
            mx = jnp.max(cand, axis=0, keepdims=True)
            am = jnp.min(jnp.where(cand == mx, row, float(CAND_ROWS)), axis=0, keepdims=True)
            hit = row == am
            best.append(mx)
            e1.append(jnp.max(jnp.where(hit, c1, -1.0), axis=0, keepdims=True))
            e2.append(jnp.max(jnp.where(hit, c2, -1.0), axis=0, keepdims=True))
            cand = jnp.where(hit, -jnp.inf, cand)
        top = best[0]
        exps = [jnp.exp(v - top) for v in best]
        denom = exps[0]
        for v in exps[1:]:
            denom = denom + v
        for i in range(P_TOPK):
            slot = hd * P_TOPK + i
            i1_ref[slot:slot + 1, :] = e1[i].astype(I32)
            i2_ref[slot:slot + 1, :] = e2[i].astype(I32)
            g_ref[slot:slot + 1, :] = exps[i] / denom


def _peer_route(h2, w_query, sub_keys, tm):
    m, d = h2.shape
    qd = w_query.shape[1]
    slot_spec = pl.BlockSpec((P_SLOTS, tm), lambda i: (0, i))
    return pl.pallas_call(
        _peer_route_kernel,
        grid=(m // tm,),
        in_specs=[pl.BlockSpec((tm, d), lambda i: (i, 0)),
                  pl.BlockSpec((d, qd), lambda i: (0, 0)),
                  pl.BlockSpec(sub_keys.shape, lambda i: (0, 0, 0))],
        out_specs=[slot_spec, slot_spec, slot_spec],
        out_shape=[jax.ShapeDtypeStruct((P_SLOTS, m), I32),
                   jax.ShapeDtypeStruct((P_SLOTS, m), I32),
                   jax.ShapeDtypeStruct((P_SLOTS, m), F32)],
        scratch_shapes=[pltpu.VMEM((P_TOPK, tm), F32)] * 4 + [pltpu.VMEM((CAND_ROWS, tm), F32)] * 3,
        compiler_params=_params("parallel"),
        name="peer_route",
    )(h2, w_query, sub_keys)


SUBLANES = 8


def _peer_mask_kernel(i1_ref, i2_ref, g_ref, o_ref):
    key = lax.broadcasted_iota(I32, (P_NKEYS, P_SLOTS), 0)

    def body(grp, carry):
        base = pl.multiple_of(grp * SUBLANES, SUBLANES)
        mats = []
        for u in range(SUBLANES):
            i1 = i1_ref[pl.ds(base + u, 1), :]
            i2 = i2_ref[pl.ds(base + u, 1), :]
            g = g_ref[pl.ds(base + u, 1), :]
            g_hi = g.astype(BF16).astype(F32)
            g_lo = g - g_hi
            hit1 = key == i1
            a = jnp.concatenate([jnp.where(hit1, g_hi, 0.0).astype(BF16),
                                 jnp.where(hit1, g_lo, 0.0).astype(BF16)], axis=1)
            onehot2 = jnp.where(key == i2, 1.0, 0.0).astype(BF16)
            bm = jnp.concatenate([onehot2, onehot2], axis=1)
            mats.append(lax.dot_general(a, bm, NT_DIMS, preferred_element_type=F32))
        o_ref[grp] = jnp.swapaxes(jnp.stack(mats, axis=0), 0, 1)
        return carry

    lax.fori_loop(0, o_ref.shape[0], body, 0)


def _peer_mask(i1, i2, gates, tt):
    m = i1.shape[0]
    slot_spec = pl.BlockSpec((tt, P_SLOTS), lambda i: (i, 0))
    return pl.pallas_call(
        _peer_mask_kernel,
        grid=(m // tt,),
        in_specs=[slot_spec, slot_spec, slot_spec],
        out_specs=pl.BlockSpec((tt // SUBLANES, P_NKEYS, SUBLANES, P_NKEYS), lambda i: (i, 0, 0, 0)),
        out_shape=jax.ShapeDtypeStruct((m // SUBLANES, P_NKEYS, SUBLANES, P_NKEYS), F32),
        compiler_params=_params("parallel"),
        name="peer_mask",
    )(i1, i2, gates)


def _peer_mlp_kernel(h_ref, dn_ref, up_ref, m_ref, x_ref, g2_ref, o_ref, acc_ref):
    e = pl.program_id(1)

    @pl.when(e == 0)
    def _():
        acc_ref[...] = jnp.zeros(acc_ref.shape, F32)

    pre = lax.dot_general(h_ref[...], dn_ref[...], NT_DIMS, preferred_element_type=F32)
    act = 0.5 * pre * (1.0 + lax.erf(pre * (0.5 ** 0.5)))
    tm = act.shape[0]
    parts = []
    for a in range(m_ref.shape[1]):
        gate = m_ref[:, a].reshape(tm, P_NKEYS)
        parts.append((act[:, a * P_NKEYS:(a + 1) * P_NKEYS] * gate).astype(BF16))
    w = jnp.concatenate(parts, axis=1)
    acc_ref[...] += jnp.dot(w, up_ref[...], preferred_element_type=F32)

    @pl.when(e == pl.num_programs(1) - 1)
    def _():
        o_ref[...] = x_ref[...] + g2_ref[...] * acc_ref[...]


def _peer_mlp(h2, down, up, mask, x1, gate, tm, te, tiles_per_mod):
    m, d = h2.shape
    n_exp = down.shape[0]
    mod_rows = gate.shape[1]
    assert te % P_NKEYS == 0 and tm % SUBLANES == 0
    return pl.pallas_call(
        _peer_mlp_kernel,
        grid=(m // tm, n_exp // te),
        in_specs=[pl.BlockSpec((tm, d), lambda i, e: (i, 0), pipeline_mode=pl.Buffered(1)),
                  pl.BlockSpec((te, d), lambda i, e: (e, 0)),
                  pl.BlockSpec((te, d), lambda i, e: (e, 0)),
                  pl.BlockSpec((tm // SUBLANES, te // P_NKEYS, SUBLANES, P_NKEYS),
                               lambda i, e: (i, e, 0, 0)),
                  pl.BlockSpec((tm, d), lambda i, e: (i, 0), pipeline_mode=pl.Buffered(1)),
                  pl.BlockSpec((None, mod_rows, d), lambda i, e: (i // tiles_per_mod, 0, 0))],
        out_specs=pl.BlockSpec((tm, d), lambda i, e: (i, 0)),
        out_shape=jax.ShapeDtypeStruct((m, d), F32),
        scratch_shapes=[pltpu.VMEM((tm, d), F32)],
        compiler_params=_params("parallel", "arbitrary"),
        name="peer_mlp",
    )(h2, down, up, mask, x1, gate)


def _alibi_slope_rows():
    slopes = jnp.exp2(-8.0 * jnp.arange(1, A_HEADS + 1, dtype=F32) / A_HEADS)
    return jnp.broadcast_to(slopes[:, None, None], (A_HEADS, 1, LANES))


def _layer(x, mod, pool_k, pool_v, layer, page_table, s0, w, *, tm, tm_mlp, side=None, kmean=None):
    bsz, seq, d = x.shape
    m = bsz * seq
    xf = x.reshape(m, d)
    chunks = jnp.split(mod, 6, axis=-1)
    if seq % tm == 0:
        mods = [c[:, None, :] for c in chunks]
        tiles_per_mod = seq // tm
        tiles_per_mod_mlp = seq // tm_mlp
    else:
        assert m == tm == tm_mlp
        mods = [jnp.repeat(c, seq, axis=0)[None] for c in chunks]
        tiles_per_mod = tiles_per_mod_mlp = 1
    sh1, sc1, g1, sh2, sc2, g2 = mods

    proj = functools.partial(_inproj, xf, w["norm1_w"], sh1, sc1, w["w_in"], tm=tm,
                             tiles_per_mod=tiles_per_mod)
    qa = proj(col0=0, ncols=A_WIDTH, aux=w["q_norm_w"], mode="headnorm")
    ka = proj(col0=A_WIDTH, ncols=A_WIDTH, aux=w["k_norm_w"], mode="headnorm")
    va = proj(col0=2 * A_WIDTH, ncols=A_WIDTH, aux=w["ones_a"], mode="scale")
    r = proj(col0=3 * A_WIDTH, ncols=2 * R_QK_WIDTH + 2 * R_WIDTH, aux=w["r_scale"], mode="scale")

    slopes = _alibi_slope_rows()
    if pool_k is None:
        side_pool, side_table = side if side is not None else (None, None)
        out_a, side_kmean = _moba_prompt(qa, ka, va, slopes, bsz, seq, side_pool, layer, side_table)
        out_r, s_new = _retention_prompt(r, s0, w["ret_norm_w"], bsz, seq)
    else:
        side_kmean = None
        if kmean is None:
            kmean = _page_means(pool_k, layer, page_table)
        q3 = qa.reshape(bsz, seq, A_WIDTH)
        q_rep = jnp.broadcast_to(q3[:, None], (bsz, A_HEADS, seq, A_WIDTH)).reshape(bsz, A_HEADS * seq, A_WIDTH)
        choice = _block_choice(q_rep, kmean.reshape(bsz, kmean.shape[1], A_WIDTH))
        blocks = choice[:, :, :MOBA_TOPK].reshape(-1)
        out_a = _moba_paged(q3, ka.reshape(bsz, seq, A_WIDTH), va.reshape(bsz, seq, A_WIDTH),
                            pool_k, pool_v, layer, page_table, blocks, slopes).reshape(m, A_WIDTH)
        out_r, s_new = _retention_step(r, s0, w["ret_norm_w"], bsz, seq)

    tm_small = min(tm, 256)
    x1, h2 = _outproj(out_a, out_r, xf, g1, sh2, sc2, w["norm2_w"], w["w_out"], tm_small,
                      tiles_per_mod * (tm // tm_small))

    i1, i2, gates = _peer_route(h2, w["peer_w_query"], w["peer_sub_keys"], tm=tm_small)
    mask = _peer_mask(i1.T, i2.T, gates.T, tt=min(m, 64))
    y = _peer_mlp(h2, w["peer_down"], w["peer_up"], mask, x1, g2,
                  tm=tm_mlp, te=1024, tiles_per_mod=tiles_per_mod_mlp)
    k_out = ka.reshape(bsz, seq, A_HEADS, A_HEAD_DIM)
    v_out = va.reshape(bsz, seq, A_HEADS, A_HEAD_DIM)
    return y.reshape(bsz, seq, d), k_out, v_out, s_new, side_kmean


def kernel(x_prompt, x_sample, cache_k, cache_v, state_ret, page_table, c_prompt, c_sample, w_ada, b_ada, norm1_w, w_in, q_norm_w, k_norm_w, ret_norm_w, w_out, norm2_w, peer_w_query, peer_sub_keys, peer_down, peer_up):
    depth = w_ada.shape[0]
    n_prompt = c_prompt.shape[0]
    n_sample = c_sample.shape[0]
    pad = (-(n_prompt + n_sample)) % 8
    d = x_prompt.shape[-1]
    hp, hs = x_prompt, x_sample
    s0_prompt = jnp.zeros((n_prompt, R_HEADS, R_QK_DIM, R_V_DIM), F32)
    r_scale = jnp.concatenate([jnp.ones((1, R_QK_WIDTH), F32),
                               jnp.full((1, R_QK_WIDTH), R_QK_DIM ** -0.5, F32),
                               jnp.ones((1, 2 * R_WIDTH), F32)], axis=1)
    outs = [[] for _ in range(6)]
    for l in range(depth):
        c_all = jnp.concatenate([c_prompt, c_sample, jnp.zeros((pad, d), F32)], axis=0)
        mod = _ada(c_all, w_ada[l], b_ada[l][None])
        w = dict(norm1_w=norm1_w[l][None], w_in=w_in[l].astype(BF16), q_norm_w=q_norm_w[l][None],
                 k_norm_w=k_norm_w[l][None], ret_norm_w=ret_norm_w[l][None],
                 w_out=w_out[l].astype(BF16), norm2_w=norm2_w[l][None],
                 peer_w_query=peer_w_query[l].astype(BF16),
                 peer_sub_keys=peer_sub_keys[l].reshape(2 * P_HEADS, P_NKEYS, -1).astype(BF16),
                 peer_down=peer_down[l].astype(BF16), peer_up=peer_up[l].astype(BF16),
                 ones_a=jnp.ones((1, A_WIDTH), F32), r_scale=r_scale)
        hp, k1, v1, s1, kmean = _layer(hp, mod[:n_prompt], None, None, l, None, s0_prompt, w, tm=512,
                                       tm_mlp=512, side=(cache_k, page_table))
        n_new = hs.shape[0] * hs.shape[1]
        hs, k2, v2, s2, _ = _layer(hs, mod[n_prompt:n_prompt + n_sample], cache_k, cache_v, l, page_table,
                                   state_ret[l], w, kmean=kmean, tm=n_new, tm_mlp=n_new)
        for lst, val in zip(outs, (k1, v1, s1, k2, v2, s2)):
            lst.append(val)
    return (hp, hs) + tuple(jnp.stack(o) for o in outs)
```

```python
import functools
import math

import jax
import jax.numpy as jnp
from jax import lax
from jax.experimental import pallas as pl
from jax.experimental.pallas import tpu as pltpu

F32 = jnp.float32
BF16 = jnp.bfloat16
I32 = jnp.int32

EPS = 1e-6
NEG = -1e30

A_HEADS = 8
A_HEAD_DIM = 128
A_WIDTH = A_HEADS * A_HEAD_DIM
MOBA_BLOCK = 256
MOBA_TOPK = 3
R_HEADS = 8
R_QK_DIM = 64
R_V_DIM = 128
R_QK_WIDTH = R_HEADS * R_QK_DIM
R_WIDTH = R_HEADS * R_V_DIM
RET_CHUNK = 256
P_HEADS = 8
P_NKEYS = 128
P_TOPK = 16
P_SLOTS = P_HEADS * P_TOPK

LANES = 128
VMEM_LIMIT_BYTES = 48 * 1024 * 1024

NT_DIMS = (((1,), (1,)), ((), ()))


def _params(*semantics):
    return pltpu.CompilerParams(dimension_semantics=semantics, vmem_limit_bytes=VMEM_LIMIT_BYTES)


def _silu(x):
    return x * jax.nn.sigmoid(x)


def _ada_kernel(c_ref, w_ref, b_ref, o_ref):
    s = _silu(c_ref[...]).astype(BF16)
    o_ref[...] = jnp.dot(s, w_ref[...].astype(BF16), preferred_element_type=F32) + b_ref[...]


def _ada(c, w, b):
    rows, d = c.shape
    n = w.shape[1]
    tn = 1024
    return pl.pallas_call(
        _ada_kernel,
        grid=(n // tn,),
        in_specs=[pl.BlockSpec((rows, d), lambda j: (0, 0)),
                  pl.BlockSpec((d, tn), lambda j: (0, j)),
                  pl.BlockSpec((1, tn), lambda j: (0, j))],
        out_specs=pl.BlockSpec((rows, tn), lambda j: (0, j)),
        out_shape=jax.ShapeDtypeStruct((rows, n), F32),
        compiler_params=_params("parallel"),
        name="ada",
    )(c, w, b)


def _inproj_kernel(x_ref, nw_ref, sh_ref, sc_ref, w_ref, aux_ref, o_ref, *, mode):
    x = x_ref[...]
    var = jnp.mean(x * x, axis=-1, keepdims=True)
    h = x * lax.rsqrt(var + EPS) * nw_ref[...]
    h = h * (1 + sc_ref[...]) + sh_ref[...]
    y = jnp.dot(h.astype(BF16), w_ref[...], preferred_element_type=F32)
    if mode == "headnorm":
        for hd in range(y.shape[1] // A_HEAD_DIM):
            yh = y[:, hd * A_HEAD_DIM:(hd + 1) * A_HEAD_DIM]
            v = jnp.mean(yh * yh, axis=-1, keepdims=True)
            o_ref[:, hd * A_HEAD_DIM:(hd + 1) * A_HEAD_DIM] = yh * lax.rsqrt(v + EPS) * aux_ref[...]
    elif mode == "scale":
        o_ref[...] = y * aux_ref[...]
    else:
        o_ref[...] = y


def _inproj(x, norm_w, shift, scale, w, col0, ncols, aux, mode, tm, tiles_per_mod):
    m, d = x.shape
    tn = 1024
    assert ncols % tn == 0 and col0 % tn == 0 and m % tm == 0
    mod_rows = shift.shape[1]
    mod_spec = pl.BlockSpec((None, mod_rows, d), lambda i, j: (i // tiles_per_mod, 0, 0))
    if mode == "headnorm":
        aux_spec = pl.BlockSpec((1, A_HEAD_DIM), lambda i, j: (0, 0))
    else:
        aux_spec = pl.BlockSpec((1, tn), lambda i, j: (0, j))
    return pl.pallas_call(
        functools.partial(_inproj_kernel, mode=mode),
        grid=(m // tm, ncols // tn),
        in_specs=[pl.BlockSpec((tm, d), lambda i, j: (i, 0)),
                  pl.BlockSpec((1, d), lambda i, j: (0, 0)),
                  mod_spec, mod_spec,
                  pl.BlockSpec((d, tn), lambda i, j: (0, col0 // tn + j)),
                  aux_spec],
        out_specs=pl.BlockSpec((tm, tn), lambda i, j: (i, j)),
        out_shape=jax.ShapeDtypeStruct((m, ncols), F32),
        compiler_params=_params("parallel", "arbitrary"),
        name="inproj_" + mode,
    )(x, norm_w, shift, scale, w, aux)


RING_SLOTS = 3


def _ring_prefetch(step, n_steps, n_slots, copies_of):
    slot = step % n_slots

    @pl.when(step == 0)
    def _():
        for ahead in range(n_slots - 1):
            @pl.when(ahead < n_steps)
            def _(ahead=ahead):
                for c in copies_of(ahead, ahead):
                    c.start()

    @pl.when(step + (n_slots - 1) < n_steps)
    def _():
        for c in copies_of(step + (n_slots - 1), (step + (n_slots - 1)) % n_slots):
            c.start()

    for c in copies_of(step, slot):
        c.wait()
    return slot


def _block_means(page_refs, o_ref):
    pages_per_block = len(page_refs) // o_ref.shape[0]
    for j in range(o_ref.shape[0]):
        acc = None
        rows = 0
        for u in range(pages_per_block):
            page = page_refs[j * pages_per_block + u]
            part = jnp.sum(page[...], axis=0)
            rows += page.shape[0]
            acc = part if acc is None else acc + part
        o_ref[j] = acc * (1.0 / rows)


def _moba_prompt_kernel(pt_ref, slope_ref, q_ref, k_ref, v_ref, *refs, seq, n_side_pages, layer):
    if n_side_pages:
        pool_ref, o_ref, side_ref, km_ref, vt_ref, pbuf, sem = refs
        step = (pl.program_id(0) * pl.num_programs(1) + pl.program_id(1)) * pl.num_programs(2) + pl.program_id(2)
        n_steps = pl.num_programs(0) * pl.num_programs(1) * pl.num_programs(2)
        steps_per_seq = pt_ref.shape[1] // n_side_pages

        def page_copies(step_i, slot_i):
            seq_i = step_i // steps_per_seq
            page0 = (step_i % steps_per_seq) * n_side_pages
            return [pltpu.make_async_copy(pool_ref.at[layer, pt_ref[seq_i, page0 + u]], pbuf.at[slot_i, u],
                                          sem.at[slot_i]) for u in range(n_side_pages)]

        slot = _ring_prefetch(step, n_steps, pbuf.shape[0], page_copies)
        _block_means([pbuf.at[slot, u] for u in range(n_side_pages)], side_ref)
    else:
        o_ref, km_ref, vt_ref = refs
    qi = pl.program_id(2)
    blk = MOBA_BLOCK
    nb = seq // blk

    @pl.when(qi == 0)
    def _():
        km_ref[...] = jnp.mean(k_ref[...].reshape(nb, blk, A_HEAD_DIM), axis=1)
        vt_ref[...] = v_ref[...].T.astype(BF16)

    q_t = q_ref[...].T
    gate = jnp.dot(km_ref[...], q_t, precision=lax.Precision.HIGHEST,
                   preferred_element_type=F32)
    g = [jnp.where(n < qi, gate[n:n + 1, :], NEG) for n in range(nb)]
    scale = A_HEAD_DIM ** -0.5
    slope = slope_ref[...][:, :1]
    col_terms = []
    for n in range(nb):
        cnt = jnp.zeros((1, blk), F32)
        for m in range(nb):
            if m < n:
                cnt = cnt + jnp.where(g[m] >= g[n], 1.0, 0.0)
            elif m > n:
                cnt = cnt + jnp.where(g[m] > g[n], 1.0, 0.0)
        chosen = jnp.where(n < qi, jnp.where(cnt < MOBA_TOPK, 1.0, 0.0), jnp.where(n == qi, 1.0, 0.0))
        ahead = ((qi - n) * blk).astype(F32)
        col_terms.append(jnp.where(chosen > 0.5, -slope * ahead, NEG))

    rc = (lax.broadcasted_iota(I32, (blk, blk), 1) - lax.broadcasted_iota(I32, (blk, blk), 0))
    bias_past = -slope * rc.astype(F32)
    bias_own = jnp.where(rc >= 0, bias_past, NEG)
    q_tb = q_t.astype(BF16)

    def attend(n_keys_blocks):
        pieces = []
        for n in range(n_keys_blocks):
            kb = k_ref[n * blk:(n + 1) * blk, :].astype(BF16)
            s_t = jnp.dot(kb, q_tb, preferred_element_type=F32)
            pieces.append(s_t * scale + jnp.where(n == qi, bias_own, bias_past) + col_terms[n])
        logits = jnp.concatenate(pieces, axis=0)
        mx = jnp.max(logits, axis=0, keepdims=True)
        p = jnp.exp(logits - mx)
        denom = jnp.sum(p, axis=0, keepdims=True)
        out_t = jnp.dot(vt_ref[:, :n_keys_blocks * blk], p.astype(BF16), preferred_element_type=F32)
        o_ref[...] = (out_t / denom).T.astype(o_ref.dtype)

    half = nb // 2
    if half == 0:
        attend(nb)
    else:
        @pl.when(qi < half)
        def _():
            attend(half)

        @pl.when(qi >= half)
        def _():
            attend(nb)


def _side_pages_per_step(side_pool, side_page_table, n_steps):
    if side_pool is None:
        return 0
    bsz, n_pages = side_page_table.shape
    pages_per_block = MOBA_BLOCK // side_pool.shape[2]
    if (bsz * n_pages) % n_steps:
        return 0
    per_step = bsz * n_pages // n_steps
    if per_step % pages_per_block or n_pages % per_step or per_step > MAX_PAGES_PER_STEP:
        return 0
    return per_step


def _moba_prompt(qa, ka, va, slopes, batch, seq, side_pool=None, layer=0, side_page_table=None):
    blk = MOBA_BLOCK
    nq = seq // blk
    n_side = _side_pages_per_step(side_pool, side_page_table, batch * A_HEADS * nq)
    in_specs = [pl.BlockSpec((None, 1, LANES), lambda b, h, i, pt: (h, 0, 0)),
                pl.BlockSpec((blk, A_HEAD_DIM), lambda b, h, i, pt: (b * nq + i, h)),
                pl.BlockSpec((seq, A_HEAD_DIM), lambda b, h, i, pt: (b, h)),
                pl.BlockSpec((seq, A_HEAD_DIM), lambda b, h, i, pt: (b, h))]
    out_specs = [pl.BlockSpec((blk, A_HEAD_DIM), lambda b, h, i, pt: (b * nq + i, h))]
    out_shape = [jax.ShapeDtypeStruct((batch * seq, A_WIDTH), BF16)]
    operands = [slopes, qa, ka, va]
    if n_side:
        page_size = side_pool.shape[2]
        side_bsz, n_pages = side_page_table.shape
        steps_per_seq = n_pages // n_side
        blocks_per_step = n_side // (MOBA_BLOCK // page_size)

        def flat(b, h, i):
            return (b * A_HEADS + h) * nq + i

        in_specs.append(pl.BlockSpec(memory_space=pl.ANY))
        operands.append(side_pool)
        out_specs.append(pl.BlockSpec(
            (None, blocks_per_step, A_HEADS, A_HEAD_DIM),
            lambda b, h, i, pt: (flat(b, h, i) // steps_per_seq, flat(b, h, i) % steps_per_seq, 0, 0)))
        out_shape.append(jax.ShapeDtypeStruct(
            (side_bsz, n_pages // (MOBA_BLOCK // page_size), A_HEADS, A_HEAD_DIM), F32))
        prefetch = side_page_table
        scratch = [pltpu.VMEM((RING_SLOTS, n_side, page_size, A_HEADS, A_HEAD_DIM), F32),
                   pltpu.SemaphoreType.DMA((RING_SLOTS,))]
        semantics = ("arbitrary", "arbitrary", "arbitrary")
    else:
        prefetch = jnp.zeros((1, 1), I32)
        scratch = []
        semantics = ("parallel", "parallel", "arbitrary")
    outs = pl.pallas_call(
        functools.partial(_moba_prompt_kernel, seq=seq, n_side_pages=n_side, layer=layer),
        grid_spec=pltpu.PrefetchScalarGridSpec(
            num_scalar_prefetch=1,
            grid=(batch, A_HEADS, nq),
            in_specs=in_specs,
            out_specs=out_specs,
            scratch_shapes=[pltpu.VMEM((nq, A_HEAD_DIM), F32),
                            pltpu.VMEM((A_HEAD_DIM, seq), BF16)]
            + scratch),
        out_shape=out_shape,
        compiler_params=_params(*semantics),
        name="moba_prompt",
    )(prefetch, *operands)
    return outs[0], (outs[1] if n_side else None)


MAX_PAGES_PER_STEP = 16


def _page_mean_kernel(pt_ref, *refs):
    del pt_ref
    _block_means(refs[:-1], refs[-1])


def _page_means(pool, layer, page_table):
    page_size = pool.shape[2]
    bsz, n_pages = page_table.shape
    pages_per_block = MOBA_BLOCK // page_size
    pages_per_step = math.gcd(n_pages, MAX_PAGES_PER_STEP)
    assert pages_per_step % pages_per_block == 0
    blocks_per_step = pages_per_step // pages_per_block
    n_blk = n_pages // pages_per_block

    def page_spec(u):
        return pl.BlockSpec((None, None, page_size, A_HEADS, A_HEAD_DIM),
                            lambda b, g, pt: (layer, pt[b, g * pages_per_step + u], 0, 0, 0))

    return pl.pallas_call(
        _page_mean_kernel,
        grid_spec=pltpu.PrefetchScalarGridSpec(
            num_scalar_prefetch=1,
            grid=(bsz, n_pages // pages_per_step),
            in_specs=[page_spec(u) for u in range(pages_per_step)],
            out_specs=pl.BlockSpec((None, blocks_per_step, A_HEADS, A_HEAD_DIM),
                                   lambda b, g, pt: (b, g, 0, 0))),
        out_shape=jax.ShapeDtypeStruct((bsz, n_blk, A_HEADS, A_HEAD_DIM), F32),
        compiler_params=_params("parallel", "arbitrary"),
        name="page_means",
    )(page_table, *([pool] * pages_per_step))


def _block_choice_kernel(q_ref, km_ref, o_ref):
    rows = q_ref.shape[0]
    t_new = rows // A_HEADS
    n_blk = km_ref.shape[0]
    q = q_ref[...]
    r_head = lax.broadcasted_iota(I32, q.shape, 0) // t_new
    c_head = lax.broadcasted_iota(I32, q.shape, 1) // A_HEAD_DIM
    qbd = jnp.where(r_head == c_head, q, 0.0)
    gate = lax.dot_general(qbd, km_ref[...], NT_DIMS, precision=lax.Precision.HIGHEST,
                           preferred_element_type=F32)
    lane = lax.broadcasted_iota(I32, gate.shape, 1).astype(F32)
    out_lane = lax.broadcasted_iota(I32, (rows, LANES), 1)
    res = jnp.zeros((rows, LANES), F32)
    for it in range(MOBA_TOPK):
        mx = jnp.max(gate, axis=1, keepdims=True)
        idx = jnp.min(jnp.where(gate == mx, lane, float(n_blk)), axis=1, keepdims=True)
        res = jnp.where(out_lane == it, idx, res)
        gate = jnp.where(lane == idx, -jnp.inf, gate)
    o_ref[...] = res.astype(I32)


def _block_choice(q_rep, kmean):
    bsz, rows, _ = q_rep.shape
    n_blk = kmean.shape[1]
    return pl.pallas_call(
        _block_choice_kernel,
        grid=(bsz,),
        in_specs=[pl.BlockSpec((None, rows, A_WIDTH), lambda b: (b, 0, 0)),
                  pl.BlockSpec((None, n_blk, A_WIDTH), lambda b: (b, 0, 0))],
        out_specs=pl.BlockSpec((None, rows, LANES), lambda b: (b, 0, 0)),
        out_shape=jax.ShapeDtypeStruct((bsz, rows, LANES), I32),
        compiler_params=_params("parallel"),
        name="block_choice",
    )(q_rep, kmean)


def _moba_paged_kernel(pt_ref, blk_ref, slope_ref, q_ref, kn_ref, vn_ref, pk_ref, pv_ref, o_ref,
                       kbuf, vbuf, sem, *, layer, t_new, page_size, past_len):
    pages_per_block = MOBA_BLOCK // page_size
    tiles_per_q = MOBA_TOPK * pages_per_block
    n_heads = pl.num_programs(1)
    step = pl.program_id(0) * n_heads + pl.program_id(1)
    n_steps = pl.num_programs(0) * n_heads

    def chosen_block(step_i, t, j):
        return blk_ref[(step_i * t_new + t) * MOBA_TOPK + j]

    def tile_copies(step_i, slot_i):
        b = step_i // n_heads
        h = step_i % n_heads
        copies = []
        for t in range(t_new):
            for j in range(MOBA_TOPK):
                block = chosen_block(step_i, t, j)
                for half in range(pages_per_block):
                    page = pt_ref[b, block * pages_per_block + half]
                    idx = t * tiles_per_q + j * pages_per_block + half
                    copies.append(pltpu.make_async_copy(pk_ref.at[layer, page, :, h, :],
                                                        kbuf.at[slot_i, idx], sem.at[slot_i]))
                    copies.append(pltpu.make_async_copy(pv_ref.at[layer, page, :, h, :],
                                                        vbuf.at[slot_i, idx], sem.at[slot_i]))
        return copies

    slot = _ring_prefetch(step, n_steps, kbuf.shape[0], tile_copies)

    scale = A_HEAD_DIM ** -0.5
    slope = slope_ref[...][:, :1]
    row = lax.broadcasted_iota(I32, (page_size, 1), 0)
    row_new = lax.broadcasted_iota(I32, (t_new, 1), 0)
    kn = kn_ref[...]
    vn = vn_ref[...]
    for t in range(t_new):
        qt = q_ref[t:t + 1, :]
        cols = []
        for j in range(MOBA_TOPK):
            block = chosen_block(step, t, j)
            for half in range(pages_per_block):
                kt = kbuf[slot, t * tiles_per_q + j * pages_per_block + half]
                lg = jnp.sum(kt * qt, axis=-1, keepdims=True) * scale
                dist = (past_len + t) - (block * MOBA_BLOCK + half * page_size + row)
                cols.append(lg - slope * dist.astype(F32))
        dist_new = t - row_new
        lg_new = jnp.sum(kn * qt, axis=-1, keepdims=True) * scale - slope * dist_new.astype(F32)
        lg_new = jnp.where(dist_new >= 0, lg_new, NEG)
        mx = jnp.max(lg_new, axis=0, keepdims=True)
        for c in cols:
            mx = jnp.maximum(mx, jnp.max(c, axis=0, keepdims=True))
        p_new = jnp.exp(lg_new - mx)
        denom = jnp.sum(p_new, axis=0, keepdims=True)
        acc = jnp.sum(p_new * vn, axis=0, keepdims=True)
        for idx, c in enumerate(cols):
            p = jnp.exp(c - mx)
            denom = denom + jnp.sum(p, axis=0, keepdims=True)
            acc = acc + jnp.sum(p * vbuf[slot, t * tiles_per_q + idx], axis=0, keepdims=True)
        o_ref[t:t + 1, :] = acc / denom


def _moba_paged(qa, ka, va, pool_k, pool_v, layer, page_table, blocks, slopes):
    bsz, t_new, _ = qa.shape
    page_size = pool_k.shape[2]
    n_pages = page_table.shape[1]
    past_len = n_pages * page_size
    n_tiles = t_new * MOBA_TOPK * (MOBA_BLOCK // page_size)
    new_spec = pl.BlockSpec((None, t_new, A_HEAD_DIM), lambda b, h, pt, blk: (b, 0, h))
    hbm_spec = pl.BlockSpec(memory_space=pl.ANY)
    return pl.pallas_call(
        functools.partial(_moba_paged_kernel, layer=layer, t_new=t_new, page_size=page_size,
                          past_len=past_len),
        grid_spec=pltpu.PrefetchScalarGridSpec(
            num_scalar_prefetch=2,
            grid=(bsz, A_HEADS),
            in_specs=[pl.BlockSpec((None, 1, LANES), lambda b, h, pt, blk: (h, 0, 0)),
                      new_spec, new_spec, new_spec, hbm_spec, hbm_spec],
            out_specs=new_spec,
            scratch_shapes=[pltpu.VMEM((RING_SLOTS, n_tiles, page_size, A_HEAD_DIM), F32),
                            pltpu.VMEM((RING_SLOTS, n_tiles, page_size, A_HEAD_DIM), F32),
                            pltpu.SemaphoreType.DMA((RING_SLOTS,))]),
        out_shape=jax.ShapeDtypeStruct((bsz, t_new, A_WIDTH), F32),
        compiler_params=_params("arbitrary", "arbitrary"),
        name="moba_paged",
    )(page_table, blocks, slopes, qa, ka, va, pool_k, pool_v)


def _ret_epilogue(o, g, nw):
    var = jnp.mean(o * o, axis=-1, keepdims=True)
    return (o * lax.rsqrt(var + EPS) * nw * _silu(g)).astype(BF16)


def _ret_chunk_kernel(q_ref, k_ref, v_ref, g_ref, din_ref, qd_ref, kd_ref, cd_ref, s0_ref, nw_ref,
                      o_ref, s_ref):
    @pl.when(pl.program_id(2) == 0)
    def _():
        s_ref[...] = s0_ref[...]

    k_t = (k_ref[...] * kd_ref[...]).T
    for hh in range(2):
        qb = q_ref[:, hh * R_QK_DIM:(hh + 1) * R_QK_DIM].astype(BF16)
        kb = k_ref[:, hh * R_QK_DIM:(hh + 1) * R_QK_DIM].astype(BF16)
        vb = v_ref[:, hh * R_V_DIM:(hh + 1) * R_V_DIM].astype(BF16)
        s = s_ref[hh]
        inner = lax.dot_general(qb, kb, NT_DIMS, preferred_element_type=F32) * din_ref[hh]
        o = (jnp.dot(inner.astype(BF16), vb, preferred_element_type=F32)
             + jnp.dot(qb, s.astype(BF16), preferred_element_type=F32) * qd_ref[hh])
        kdt = k_t[hh * R_QK_DIM:(hh + 1) * R_QK_DIM, :].astype(BF16)
        s_ref[hh] = s * cd_ref[hh] + jnp.dot(kdt, vb, preferred_element_type=F32)
        o_ref[:, hh * R_V_DIM:(hh + 1) * R_V_DIM] = _ret_epilogue(
            o, g_ref[:, hh * R_V_DIM:(hh + 1) * R_V_DIM], nw_ref[...])


def _ret_tables(chunk):
    log_gamma = jnp.log1p(-jnp.exp2(-5.0 - jnp.arange(R_HEADS, dtype=F32)))
    idx = jnp.arange(chunk, dtype=F32)
    diff = idx[:, None] - idx[None, :]
    causal = diff >= 0
    decay_in = jnp.where(causal[None], jnp.exp(jnp.where(causal, diff, 0.0)[None] * log_gamma[:, None, None]), 0.0)
    q_dec = jnp.exp((idx + 1)[None, :] * log_gamma[:, None])
    k_dec = jnp.exp((chunk - 1 - idx)[None, :] * log_gamma[:, None])
    chunk_dec = jnp.exp(chunk * log_gamma)
    return decay_in, q_dec, k_dec, chunk_dec


def _pair_lanes(k_dec):
    h, c = k_dec.shape
    t = jnp.broadcast_to(k_dec.reshape(h // 2, 2, c, 1), (h // 2, 2, c, R_QK_DIM))
    return t.transpose(0, 2, 1, 3).reshape(h // 2, c, 2 * R_QK_DIM)


def _retention_prompt(r, s0, ret_norm_w, batch, seq):
    c = math.gcd(seq, RET_CHUNK)
    nc = seq // c
    decay_in, q_dec, k_dec, chunk_dec = _ret_tables(c)
    qd = q_dec[:, :, None]
    kd = _pair_lanes(k_dec)
    cd = jnp.broadcast_to(chunk_dec[:, None, None], (R_HEADS, 1, LANES))
    qk_blocks = R_QK_WIDTH // LANES
    v_off = 2 * R_QK_WIDTH // (2 * R_V_DIM)
    g_off = (2 * R_QK_WIDTH + R_WIDTH) // (2 * R_V_DIM)
    return pl.pallas_call(
        _ret_chunk_kernel,
        grid=(batch, R_HEADS // 2, nc),
        in_specs=[pl.BlockSpec((c, LANES), lambda b, p, i: (b * nc + i, p)),
                  pl.BlockSpec((c, LANES), lambda b, p, i: (b * nc + i, qk_blocks + p)),
                  pl.BlockSpec((c, 2 * R_V_DIM), lambda b, p, i: (b * nc + i, v_off + p)),
                  pl.BlockSpec((c, 2 * R_V_DIM), lambda b, p, i: (b * nc + i, g_off + p)),
                  pl.BlockSpec((2, c, c), lambda b, p, i: (p, 0, 0)),
                  pl.BlockSpec((2, c, 1), lambda b, p, i: (p, 0, 0)),
                  pl.BlockSpec((None, c, LANES), lambda b, p, i: (p, 0, 0)),
                  pl.BlockSpec((2, 1, LANES), lambda b, p, i: (p, 0, 0)),
                  pl.BlockSpec((None, 2, R_QK_DIM, R_V_DIM), lambda b, p, i: (b, p, 0, 0)),
                  pl.BlockSpec((1, R_V_DIM), lambda b, p, i: (0, 0))],
        out_specs=[pl.BlockSpec((c, 2 * R_V_DIM), lambda b, p, i: (b * nc + i, p)),
                   pl.BlockSpec((None, 2, R_QK_DIM, R_V_DIM), lambda b, p, i: (b, p, 0, 0))],
        out_shape=[jax.ShapeDtypeStruct((batch * seq, R_WIDTH), BF16),
                   jax.ShapeDtypeStruct((batch, R_HEADS, R_QK_DIM, R_V_DIM), F32)],
        compiler_params=_params("parallel", "parallel", "arbitrary"),
        name="retention_chunks",
    )(r, r, r, r, decay_in, qd, kd, cd, s0, ret_norm_w)


def _ret_step_kernel(q_ref, k_ref, v_ref, g_ref, din_ref, qd_ref, kd_ref, cd_ref, s0_ref, nw_ref,
                     o_ref, s_ref, *, t_new):
    rows = q_ref.shape[0]
    bsz = rows // t_new
    k_t = (k_ref[...] * kd_ref[...]).T
    seq_of_row = lax.broadcasted_iota(I32, (rows, bsz * R_QK_DIM), 0) // t_new
    seq_of_col = lax.broadcasted_iota(I32, (rows, bsz * R_QK_DIM), 1) // R_QK_DIM
    own_q = seq_of_row == seq_of_col
    seq_of_srow = lax.broadcasted_iota(I32, (bsz * R_QK_DIM, rows), 0) // R_QK_DIM
    seq_of_scol = lax.broadcasted_iota(I32, (bsz * R_QK_DIM, rows), 1) // t_new
    own_k = seq_of_srow == seq_of_scol
    for hh in range(2):
        q = q_ref[:, hh * R_QK_DIM:(hh + 1) * R_QK_DIM]
        qb = q.astype(BF16)
        kb = k_ref[:, hh * R_QK_DIM:(hh + 1) * R_QK_DIM].astype(BF16)
        vb = v_ref[:, hh * R_V_DIM:(hh + 1) * R_V_DIM].astype(BF16)
        s = s0_ref[:, hh].reshape(bsz * R_QK_DIM, R_V_DIM)
        inner = lax.dot_general(qb, kb, NT_DIMS, preferred_element_type=F32) * din_ref[hh]
        q_wide = jnp.where(own_q, jnp.concatenate([q] * bsz, axis=1), 0.0).astype(BF16)
        o = (jnp.dot(inner.astype(BF16), vb, preferred_element_type=F32)
             + jnp.dot(q_wide, s.astype(BF16), preferred_element_type=F32) * qd_ref[hh])
        kdt = k_t[hh * R_QK_DIM:(hh + 1) * R_QK_DIM, :]
        k_tall = jnp.where(own_k, jnp.concatenate([kdt] * bsz, axis=0), 0.0).astype(BF16)
        s_new = s * cd_ref[hh] + jnp.dot(k_tall, vb, preferred_element_type=F32)
        s_ref[:, hh] = s_new.reshape(bsz, R_QK_DIM, R_V_DIM)
        o_ref[:, hh * R_V_DIM:(hh + 1) * R_V_DIM] = _ret_epilogue(
            o, g_ref[:, hh * R_V_DIM:(hh + 1) * R_V_DIM], nw_ref[...])


def _retention_step(r, s0, ret_norm_w, batch, t_new):
    rows = batch * t_new
    decay_in, q_dec, k_dec, chunk_dec = _ret_tables(t_new)
    same_seq = jnp.eye(batch, dtype=F32)
    din = jnp.einsum("hij,ab->haibj", decay_in, same_seq).reshape(R_HEADS, rows, rows)
    qd = jnp.tile(q_dec, (1, batch))[:, :, None]
    kd = _pair_lanes(jnp.tile(k_dec, (1, batch)))
    cd = jnp.broadcast_to(chunk_dec[:, None, None], (R_HEADS, 1, LANES))
    qk_blocks = R_QK_WIDTH // LANES
    v_off = 2 * R_QK_WIDTH // (2 * R_V_DIM)
    g_off = (2 * R_QK_WIDTH + R_WIDTH) // (2 * R_V_DIM)
    return pl.pallas_call(
        functools.partial(_ret_step_kernel, t_new=t_new),
        grid=(R_HEADS // 2,),
        in_specs=[pl.BlockSpec((rows, LANES), lambda p: (0, p)),
                  pl.BlockSpec((rows, LANES), lambda p: (0, qk_blocks + p)),
                  pl.BlockSpec((rows, 2 * R_V_DIM), lambda p: (0, v_off + p)),
                  pl.BlockSpec((rows, 2 * R_V_DIM), lambda p: (0, g_off + p)),
                  pl.BlockSpec((2, rows, rows), lambda p: (p, 0, 0)),
                  pl.BlockSpec((2, rows, 1), lambda p: (p, 0, 0)),
                  pl.BlockSpec((None, rows, LANES), lambda p: (p, 0, 0)),
                  pl.BlockSpec((2, 1, LANES), lambda p: (p, 0, 0)),
                  pl.BlockSpec((batch, 2, R_QK_DIM, R_V_DIM), lambda p: (0, p, 0, 0)),
                  pl.BlockSpec((1, R_V_DIM), lambda p: (0, 0))],
        out_specs=[pl.BlockSpec((rows, 2 * R_V_DIM), lambda p: (0, p)),
                   pl.BlockSpec((batch, 2, R_QK_DIM, R_V_DIM), lambda p: (0, p, 0, 0))],
        out_shape=[jax.ShapeDtypeStruct((rows, R_WIDTH), BF16),
                   jax.ShapeDtypeStruct((batch, R_HEADS, R_QK_DIM, R_V_DIM), F32)],
        compiler_params=_params("parallel"),
        name="retention_step",
    )(r, r, r, r, din, qd, kd, cd, s0, ret_norm_w)


def _outproj_kernel(a_ref, r_ref, x_ref, g1_ref, sh_ref, sc_ref, nw_ref, wa_ref, wr_ref,
                    x1_ref, h2_ref):
    mixed = (jnp.dot(a_ref[...].astype(BF16), wa_ref[...], preferred_element_type=F32)
             + jnp.dot(r_ref[...].astype(BF16), wr_ref[...], preferred_element_type=F32))
    x1 = x_ref[...] + g1_ref[...] * mixed
    x1_ref[...] = x1
    var = jnp.mean(x1 * x1, axis=-1, keepdims=True)
    h2 = x1 * lax.rsqrt(var + EPS) * nw_ref[...]
    h2_ref[...] = (h2 * (1 + sc_ref[...]) + sh_ref[...]).astype(BF16)


def _outproj(out_a, out_r, x, gate, shift, scale, norm_w, w_out, tm, tiles_per_mod):
    m, d = x.shape
    mod_rows = shift.shape[1]
    mod_spec = pl.BlockSpec((None, mod_rows, d), lambda i: (i // tiles_per_mod, 0, 0))
    return pl.pallas_call(
        _outproj_kernel,
        grid=(m // tm,),
        in_specs=[pl.BlockSpec((tm, A_WIDTH), lambda i: (i, 0)),
                  pl.BlockSpec((tm, R_WIDTH), lambda i: (i, 0)),
                  pl.BlockSpec((tm, d), lambda i: (i, 0)),
                  mod_spec, mod_spec, mod_spec,
                  pl.BlockSpec((1, d), lambda i: (0, 0)),
                  pl.BlockSpec((A_WIDTH, d), lambda i: (0, 0)),
                  pl.BlockSpec((R_WIDTH, d), lambda i: (A_WIDTH // R_WIDTH, 0))],
        out_specs=[pl.BlockSpec((tm, d), lambda i: (i, 0)),
                   pl.BlockSpec((tm, d), lambda i: (i, 0))],
        out_shape=[jax.ShapeDtypeStruct((m, d), F32), jax.ShapeDtypeStruct((m, d), BF16)],
        compiler_params=_params("parallel"),
        name="outproj",
    )(out_a, out_r, x, gate, shift, scale, norm_w, w_out, w_out)


def _pruned_pairs():
    return [(a, P_TOPK // (a + 1)) for a in range(P_TOPK)]


N_CAND = sum(nb for _, nb in _pruned_pairs())
CAND_ROWS = -(-N_CAND // 8) * 8


def _top_rows(x, k, val_ref, idx_ref):
    nrows = x.shape[0]
    row = lax.broadcasted_iota(I32, x.shape, 0).astype(F32)
    for i in range(k):
        mx = jnp.max(x, axis=0, keepdims=True)
        am = jnp.min(jnp.where(x == mx, row, float(nrows)), axis=0, keepdims=True)
        val_ref[i:i + 1, :] = mx
        idx_ref[i:i + 1, :] = am
        x = jnp.where(row == am, -jnp.inf, x)


def _peer_route_kernel(h_ref, wq_ref, sk_ref, i1_ref, i2_ref, g_ref,
                       s1_ref, k1_ref, s2_ref, k2_ref, cand_ref, c1_ref, c2_ref):
    tm = h_ref.shape[0]
    qp = jnp.dot(h_ref[...], wq_ref[...], preferred_element_type=F32)
    cand_ref[...] = jnp.full(cand_ref.shape, -jnp.inf, F32)
    c1_ref[...] = jnp.zeros(c1_ref.shape, F32)
    c2_ref[...] = jnp.zeros(c2_ref.shape, F32)
    row = lax.broadcasted_iota(I32, (CAND_ROWS, tm), 0).astype(F32)
    for hd in range(P_HEADS):
        for half, (s_ref, k_ref) in enumerate(((s1_ref, k1_ref), (s2_ref, k2_ref))):
            col = (hd * 2 + half) * LANES
            q_part = qp[:, col:col + LANES].astype(BF16)
            scores = lax.dot_general(sk_ref[hd * 2 + half], q_part, NT_DIMS,
                                     preferred_element_type=F32)
            _top_rows(scores, P_TOPK, s_ref, k_ref)
        off = 0
        for a, nb in _pruned_pairs():
            cand_ref[off:off + nb, :] = s1_ref[a:a + 1, :] + s2_ref[0:nb, :]
            c1_ref[off:off + nb, :] = jnp.broadcast_to(k1_ref[a:a + 1, :], (nb, tm))
            c2_ref[off:off + nb, :] = k2_ref[0:nb, :]
            off += nb
        cand = cand_ref[...]
        c1 = c1_ref[...]
        c2 = c2_ref[...]
        best, e1, e2 = [], [], []
        for _ in range(P_TOPK):
            mx = jnp.max(cand, axis=0, keepdims=True)
            am = jnp.min(jnp.where(cand == mx, row, float(CAND_ROWS)), axis=0, keepdims=True)
            hit = row == am
            best.append(mx)
            e1.append(jnp.max(jnp.where(hit, c1, -1.0), axis=0, keepdims=True))
            e2.append(jnp.max(jnp.where(hit, c2, -1.0), axis=0, keepdims=True))
            cand = jnp.where(hit, -jnp.inf, cand)
        top = best[0]
        exps = [jnp.exp(v - top) for v in best]
        denom = exps[0]
        for v in exps[1:]:
            denom = denom + v
        for i in range(P_TOPK):
            slot = hd * P_TOPK + i
            i1_ref[slot:slot + 1, :] = e1[i].astype(I32)
            i2_ref[slot:slot + 1, :] = e2[i].astype(I32)
            g_ref[slot:slot + 1, :] = exps[i] / denom


def _peer_route(h2, w_query, sub_keys, tm):
    m, d = h2.shape
    qd = w_query.shape[1]
    slot_spec = pl.BlockSpec((P_SLOTS, tm), lambda i: (0, i))
    return pl.pallas_call(
        _peer_route_kernel,
        grid=(m // tm,),
        in_specs=[pl.BlockSpec((tm, d), lambda i: (i, 0)),
                  pl.BlockSpec((d, qd), lambda i: (0, 0)),
                  pl.BlockSpec(sub_keys.shape, lambda i: (0, 0, 0))],
        out_specs=[slot_spec, slot_spec, slot_spec],
        out_shape=[jax.ShapeDtypeStruct((P_SLOTS, m), I32),
                   jax.ShapeDtypeStruct((P_SLOTS, m), I32),
                   jax.ShapeDtypeStruct((P_SLOTS, m), F32)],
        scratch_shapes=[pltpu.VMEM((P_TOPK, tm), F32)] * 4 + [pltpu.VMEM((CAND_ROWS, tm), F32)] * 3,
        compiler_params=_params("parallel"),
        name="peer_route",
    )(h2, w_query, sub_keys)


SUBLANES = 8


def _peer_mask_kernel(i1_ref, i2_ref, g_ref, o_ref):
    key = lax.broadcasted_iota(I32, (P_NKEYS, P_SLOTS), 0)

    def body(grp, carry):
        base = pl.multiple_of(grp * SUBLANES, SUBLANES)
        mats = []
        for u in range(SUBLANES):
            i1 = i1_ref[pl.ds(base + u, 1), :]
            i2 = i2_ref[pl.ds(base + u, 1), :]
            g = g_ref[pl.ds(base + u, 1), :]
            g_hi = g.astype(BF16).astype(F32)
            g_lo = g - g_hi
            hit1 = key == i1
            a = jnp.concatenate([jnp.where(hit1, g_hi, 0.0).astype(BF16),
                                 jnp.where(hit1, g_lo, 0.0).astype(BF16)], axis=1)
            onehot2 = jnp.where(key == i2, 1.0, 0.0).astype(BF16)
            bm = jnp.concatenate([onehot2, onehot2], axis=1)
            mats.append(lax.dot_general(a, bm, NT_DIMS, preferred_element_type=F32))
        o_ref[grp] = jnp.swapaxes(jnp.stack(mats, axis=0), 0, 1)
        return carry

    lax.fori_loop(0, o_ref.shape[0], body, 0, unroll=2)


def _peer_mask(i1, i2, gates, tt):
    m = i1.shape[0]
    slot_spec = pl.BlockSpec((tt, P_SLOTS), lambda i: (i, 0))
    return pl.pallas_call(
        _peer_mask_kernel,
        grid=(m // tt,),
        in_specs=[slot_spec, slot_spec, slot_spec],
        out_specs=pl.BlockSpec((tt // SUBLANES, P_NKEYS, SUBLANES, P_NKEYS), lambda i: (i, 0, 0, 0)),
        out_shape=jax.ShapeDtypeStruct((m // SUBLANES, P_NKEYS, SUBLANES, P_NKEYS), F32),
        compiler_params=_params("parallel"),
        name="peer_mask",
    )(i1, i2, gates)


def _peer_mlp_kernel(h_ref, dn_ref, up_ref, m_ref, x_ref, g2_ref, o_ref, acc_ref):
    e = pl.program_id(1)

    @pl.when(e == 0)
    def _():
        acc_ref[...] = jnp.zeros(acc_ref.shape, F32)

    pre = lax.dot_general(h_ref[...], dn_ref[...], NT_DIMS, preferred_element_type=F32)
    act = 0.5 * pre * (1.0 + lax.erf(pre * (0.5 ** 0.5)))
    tm = act.shape[0]
    parts = []
    for a in range(m_ref.shape[1]):
        gate = m_ref[:, a].reshape(tm, P_NKEYS)
        parts.append((act[:, a * P_NKEYS:(a + 1) * P_NKEYS] * gate).astype(BF16))
    w = jnp.concatenate(parts, axis=1)
    acc_ref[...] += jnp.dot(w, up_ref[...], preferred_element_type=F32)

    @pl.when(e == pl.num_programs(1) - 1)
    def _():
        o_ref[...] = x_ref[...] + g2_ref[...] * acc_ref[...]


def _peer_mlp(h2, down, up, mask, x1, gate, tm, te, tiles_per_mod):
    m, d = h2.shape
    n_exp = down.shape[0]
    mod_rows = gate.shape[1]
    assert te % P_NKEYS == 0 and tm % SUBLANES == 0
    return pl.pallas_call(
        _peer_mlp_kernel,
        grid=(m // tm, n_exp // te),
        in_specs=[pl.BlockSpec((tm, d), lambda i, e: (i, 0), pipeline_mode=pl.Buffered(1)),
                  pl.BlockSpec((te, d), lambda i, e: (e, 0)),
                  pl.BlockSpec((te, d), lambda i, e: (e, 0)),
                  pl.BlockSpec((tm // SUBLANES, te // P_NKEYS, SUBLANES, P_NKEYS),
                               lambda i, e: (i, e, 0, 0)),
                  pl.BlockSpec((tm, d), lambda i, e: (i, 0), pipeline_mode=pl.Buffered(1)),
                  pl.BlockSpec((None, mod_rows, d), lambda i, e: (i // tiles_per_mod, 0, 0))],
        out_specs=pl.BlockSpec((tm, d), lambda i, e: (i, 0)),
        out_shape=jax.ShapeDtypeStruct((m, d), F32),
        scratch_shapes=[pltpu.VMEM((tm, d), F32)],
        compiler_params=_params("parallel", "arbitrary"),
        name="peer_mlp",
    )(h2, down, up, mask, x1, gate)


def _alibi_slope_rows():
    slopes = jnp.exp2(-8.0 * jnp.arange(1, A_HEADS + 1, dtype=F32) / A_HEADS)
    return jnp.broadcast_to(slopes[:, None, None], (A_HEADS, 1, LANES))


def _layer(x, mod, pool_k, pool_v, layer, page_table, s0, w, *, tm, tm_mlp, side=None, kmean=None):
    bsz, seq, d = x.shape
    m = bsz * seq
    xf = x.reshape(m, d)
    chunks = jnp.split(mod, 6, axis=-1)
    if seq % tm == 0:
        mods = [c[:, None, :] for c in chunks]
        tiles_per_mod = seq // tm
        tiles_per_mod_mlp = seq // tm_mlp
    else:
        assert m == tm == tm_mlp
        mods = [jnp.repeat(c, seq, axis=0)[None] for c in chunks]
        tiles_per_mod = tiles_per_mod_mlp = 1
    sh1, sc1, g1, sh2, sc2, g2 = mods

    proj = functools.partial(_inproj, xf, w["norm1_w"], sh1, sc1, w["w_in"], tm=tm,
                             tiles_per_mod=tiles_per_mod)
    qa = proj(col0=0, ncols=A_WIDTH, aux=w["q_norm_w"], mode="headnorm")
    ka = proj(col0=A_WIDTH, ncols=A_WIDTH, aux=w["k_norm_w"], mode="headnorm")
    va = proj(col0=2 * A_WIDTH, ncols=A_WIDTH, aux=w["ones_a"], mode="scale")
    r = proj(col0=3 * A_WIDTH, ncols=2 * R_QK_WIDTH + 2 * R_WIDTH, aux=w["r_scale"], mode="scale")

    slopes = _alibi_slope_rows()
    if pool_k is None:
        side_pool, side_table = side if side is not None else (None, None)
        out_a, side_kmean = _moba_prompt(qa, ka, va, slopes, bsz, seq, side_pool, layer, side_table)
        out_r, s_new = _retention_prompt(r, s0, w["ret_norm_w"], bsz, seq)
    else:
        side_kmean = None
        if kmean is None:
            kmean = _page_means(pool_k, layer, page_table)
        q3 = qa.reshape(bsz, seq, A_WIDTH)
        q_rep = jnp.broadcast_to(q3[:, None], (bsz, A_HEADS, seq, A_WIDTH)).reshape(bsz, A_HEADS * seq, A_WIDTH)
        choice = _block_choice(q_rep, kmean.reshape(bsz, kmean.shape[1], A_WIDTH))
        blocks = choice[:, :, :MOBA_TOPK].reshape(-1)
        out_a = _moba_paged(q3, ka.reshape(bsz, seq, A_WIDTH), va.reshape(bsz, seq, A_WIDTH),
                            pool_k, pool_v, layer, page_table, blocks, slopes).reshape(m, A_WIDTH)
        out_r, s_new = _retention_step(r, s0, w["ret_norm_w"], bsz, seq)

    tm_small = min(tm, 256)
    x1, h2 = _outproj(out_a, out_r, xf, g1, sh2, sc2, w["norm2_w"], w["w_out"], tm_small,
                      tiles_per_mod * (tm // tm_small))

    i1, i2, gates = _peer_route(h2, w["peer_w_query"], w["peer_sub_keys"], tm=tm_small)
    mask = _peer_mask(i1.T, i2.T, gates.T, tt=min(m, 64))
    y = _peer_mlp(h2, w["peer_down"], w["peer_up"], mask, x1, g2,
                  tm=tm_mlp, te=1024, tiles_per_mod=tiles_per_mod_mlp)
    k_out = ka.reshape(bsz, seq, A_HEADS, A_HEAD_DIM)
    v_out = va.reshape(bsz, seq, A_HEADS, A_HEAD_DIM)
    return y.reshape(bsz, seq, d), k_out, v_out, s_new, side_kmean


def kernel(x_prompt, x_sample, cache_k, cache_v, state_ret, page_table, c_prompt, c_sample, w_ada, b_ada, norm1_w, w_in, q_norm_w, k_norm_w, ret_norm_w, w_out, norm2_w, peer_w_query, peer_sub_keys, peer_down, peer_up):
    depth = w_ada.shape[0]
    n_prompt = c_prompt.shape[0]
    n_sample = c_sample.shape[0]
    pad = (-(n_prompt + n_sample)) % 8
    d = x_prompt.shape[-1]
    hp, hs = x_prompt, x_sample
    s0_prompt = jnp.zeros((n_prompt, R_HEADS, R_QK_DIM, R_V_DIM), F32)
    r_scale = jnp.concatenate([jnp.ones((1, R_QK_WIDTH), F32),
                               jnp.full((1, R_QK_WIDTH), R_QK_DIM ** -0.5, F32),
                               jnp.ones((1, 2 * R_WIDTH), F32)], axis=1)
    outs = [[] for _ in range(6)]
    for l in range(depth):
        c_all = jnp.concatenate([c_prompt, c_sample, jnp.zeros((pad, d), F32)], axis=0)
        mod = _ada(c_all, w_ada[l], b_ada[l][None])
        w = dict(norm1_w=norm1_w[l][None], w_in=w_in[l].astype(BF16), q_norm_w=q_norm_w[l][None],
                 k_norm_w=k_norm_w[l][None], ret_norm_w=ret_norm_w[l][None],
                 w_out=w_out[l].astype(BF16), norm2_w=norm2_w[l][None],
                 peer_w_query=peer_w_query[l].astype(BF16),
                 peer_sub_keys=peer_sub_keys[l].reshape(2 * P_HEADS, P_NKEYS, -1).astype(BF16),
                 peer_down=peer_down[l].astype(BF16), peer_up=peer_up[l].astype(BF16),
                 ones_a=jnp.ones((1, A_WIDTH), F32), r_scale=r_scale)
        hp, k1, v1, s1, kmean = _layer(hp, mod[:n_prompt], None, None, l, None, s0_prompt, w, tm=512,
                                       tm_mlp=512, side=(cache_k, page_table))
        n_new = hs.shape[0] * hs.shape[1]
        hs, k2, v2, s2, _ = _layer(hs, mod[n_prompt:n_prompt + n_sample], cache_k, cache_v, l, page_table,
                                   state_ret[l], w, kmean=kmean, tm=n_new, tm_mlp=n_new)
        for lst, val in zip(outs, (k1, v1, s1, k2, v2, s2)):
            lst.append(val)
    return (hp, hs) + tuple(jnp.stack(o) for o in outs)
```

```python
import functools
import math

import jax
import jax.numpy as jnp
from jax import lax
from jax.experimental import pallas as pl
from jax.experimental.pallas import tpu as pltpu

F32 = jnp.float32
BF16 = jnp.bfloat16
I32 = jnp.int32

EPS = 1e-6
NEG = -1e30

A_HEADS = 8
A_HEAD_DIM = 128
A_WIDTH = A_HEADS * A_HEAD_DIM
MOBA_BLOCK = 256
MOBA_TOPK = 3
R_HEADS = 8
R_QK_DIM = 64
R_V_DIM = 128
R_QK_WIDTH = R_HEADS * R_QK_DIM
R_WIDTH = R_HEADS * R_V_DIM
RET_CHUNK = 256
P_HEADS = 8
P_NKEYS = 128
P_TOPK = 16
P_SLOTS = P_HEADS * P_TOPK

LANES = 128
VMEM_LIMIT_BYTES = 48 * 1024 * 1024

NT_DIMS = (((1,), (1,)), ((), ()))


def _params(*semantics):
    return pltpu.CompilerParams(dimension_semantics=semantics, vmem_limit_bytes=VMEM_LIMIT_BYTES)


def _silu(x):
    return x * jax.nn.sigmoid(x)


def _ada_kernel(c_ref, w_ref, b_ref, o_ref):
    s = _silu(c_ref[...]).astype(BF16)
    o_ref[...] = jnp.dot(s, w_ref[...].astype(BF16), preferred_element_type=F32) + b_ref[...]


def _ada(c, w, b):
    rows, d = c.shape
    n = w.shape[1]
    tn = 1024
    return pl.pallas_call(
        _ada_kernel,
        grid=(n // tn,),
        in_specs=[pl.BlockSpec((rows, d), lambda j: (0, 0)),
                  pl.BlockSpec((d, tn), lambda j: (0, j)),
                  pl.BlockSpec((1, tn), lambda j: (0, j))],
        out_specs=pl.BlockSpec((rows, tn), lambda j: (0, j)),
        out_shape=jax.ShapeDtypeStruct((rows, n), F32),
        compiler_params=_params("parallel"),
        name="ada",
    )(c, w, b)


def _inproj_kernel(x_ref, nw_ref, sh_ref, sc_ref, w_ref, aux_ref, o_ref, *, mode):
    x = x_ref[...]
    var = jnp.mean(x * x, axis=-1, keepdims=True)
    h = x * lax.rsqrt(var + EPS) * nw_ref[...]
    h = h * (1 + sc_ref[...]) + sh_ref[...]
    y = jnp.dot(h.astype(BF16), w_ref[...], preferred_element_type=F32)
    if mode == "headnorm":
        for hd in range(y.shape[1] // A_HEAD_DIM):
            yh = y[:, hd * A_HEAD_DIM:(hd + 1) * A_HEAD_DIM]
            v = jnp.mean(yh * yh, axis=-1, keepdims=True)
            o_ref[:, hd * A_HEAD_DIM:(hd + 1) * A_HEAD_DIM] = yh * lax.rsqrt(v + EPS) * aux_ref[...]
    elif mode == "scale":
        o_ref[...] = y * aux_ref[...]
    else:
        o_ref[...] = y


def _inproj(x, norm_w, shift, scale, w, col0, ncols, aux, mode, tm, tiles_per_mod):
    m, d = x.shape
    tn = 1024
    assert ncols % tn == 0 and col0 % tn == 0 and m % tm == 0
    mod_rows = shift.shape[1]
    mod_spec = pl.BlockSpec((None, mod_rows, d), lambda i, j: (i // tiles_per_mod, 0, 0))
    if mode == "headnorm":
        aux_spec = pl.BlockSpec((1, A_HEAD_DIM), lambda i, j: (0, 0))
    else:
        aux_spec = pl.BlockSpec((1, tn), lambda i, j: (0, j))
    return pl.pallas_call(
        functools.partial(_inproj_kernel, mode=mode),
        grid=(m // tm, ncols // tn),
        in_specs=[pl.BlockSpec((tm, d), lambda i, j: (i, 0)),
                  pl.BlockSpec((1, d), lambda i, j: (0, 0)),
                  mod_spec, mod_spec,
                  pl.BlockSpec((d, tn), lambda i, j: (0, col0 // tn + j)),
                  aux_spec],
        out_specs=pl.BlockSpec((tm, tn), lambda i, j: (i, j)),
        out_shape=jax.ShapeDtypeStruct((m, ncols), F32),
        compiler_params=_params("parallel", "arbitrary"),
        name="inproj_" + mode,
    )(x, norm_w, shift, scale, w, aux)


RING_SLOTS = 3


def _ring_prefetch(step, n_steps, n_slots, copies_of):
    slot = step % n_slots

    @pl.when(step == 0)
    def _():
        for ahead in range(n_slots - 1):
            @pl.when(ahead < n_steps)
            def _(ahead=ahead):
                for c in copies_of(ahead, ahead):
                    c.start()

    @pl.when(step + (n_slots - 1) < n_steps)
    def _():
        for c in copies_of(step + (n_slots - 1), (step + (n_slots - 1)) % n_slots):
            c.start()

    for c in copies_of(step, slot):
        c.wait()
    return slot


def _ret_epilogue(o, g, nw):
    var = jnp.mean(o * o, axis=-1, keepdims=True)
    return (o * lax.rsqrt(var + EPS) * nw * _silu(g)).astype(BF16)


def _ret_chunk(q_ref, k_ref, v_ref, g_ref, din_ref, qd_ref, kd_ref, cd_ref, s0_ref, nw_ref, o_ref, s_ref,
               is_first):
    @pl.when(is_first)
    def _():
        s_ref[...] = s0_ref[...]

    k_t = (k_ref[...] * kd_ref[...]).T
    for hh in range(2):
        qb = q_ref[:, hh * R_QK_DIM:(hh + 1) * R_QK_DIM].astype(BF16)
        kb = k_ref[:, hh * R_QK_DIM:(hh + 1) * R_QK_DIM].astype(BF16)
        vb = v_ref[:, hh * R_V_DIM:(hh + 1) * R_V_DIM].astype(BF16)
        s = s_ref[hh]
        inner = lax.dot_general(qb, kb, NT_DIMS, preferred_element_type=F32) * din_ref[hh]
        o = (jnp.dot(inner.astype(BF16), vb, preferred_element_type=F32)
             + jnp.dot(qb, s.astype(BF16), preferred_element_type=F32) * qd_ref[hh])
        kdt = k_t[hh * R_QK_DIM:(hh + 1) * R_QK_DIM, :].astype(BF16)
        s_ref[hh] = s * cd_ref[hh] + jnp.dot(kdt, vb, preferred_element_type=F32)
        o_ref[:, hh * R_V_DIM:(hh + 1) * R_V_DIM] = _ret_epilogue(
            o, g_ref[:, hh * R_V_DIM:(hh + 1) * R_V_DIM], nw_ref[...])


N_RET_INPUTS = 10


def _block_means(page_refs, o_ref):
    pages_per_block = len(page_refs) // o_ref.shape[0]
    for j in range(o_ref.shape[0]):
        acc = None
        rows = 0
        for u in range(pages_per_block):
            page = page_refs[j * pages_per_block + u]
            part = jnp.sum(page[...], axis=0)
            rows += page.shape[0]
            acc = part if acc is None else acc + part
        o_ref[j] = acc * (1.0 / rows)


def _moba_prompt_kernel(pt_ref, slope_ref, q_ref, k_ref, v_ref, *refs, seq, n_side_pages, layer, with_ret):
    refs = list(refs)
    pool_ref = refs.pop(0) if n_side_pages else None
    ret_in = [refs.pop(0) for _ in range(N_RET_INPUTS)] if with_ret else []
    o_ref = refs.pop(0)
    side_ref = refs.pop(0) if n_side_pages else None
    ret_out = [refs.pop(0) for _ in range(2)] if with_ret else []
    km_ref, vt_ref = refs.pop(0), refs.pop(0)
    if n_side_pages:
        pbuf, sem = refs
        step = (pl.program_id(0) * pl.num_programs(1) + pl.program_id(1)) * pl.num_programs(2) + pl.program_id(2)
        n_steps = pl.num_programs(0) * pl.num_programs(1) * pl.num_programs(2)
        steps_per_seq = pt_ref.shape[1] // n_side_pages

        def page_copies(step_i, slot_i):
            seq_i = step_i // steps_per_seq
            page0 = (step_i % steps_per_seq) * n_side_pages
            return [pltpu.make_async_copy(pool_ref.at[layer, pt_ref[seq_i, page0 + u]], pbuf.at[slot_i, u],
                                          sem.at[slot_i]) for u in range(n_side_pages)]

        slot = _ring_prefetch(step, n_steps, pbuf.shape[0], page_copies)
        _block_means([pbuf.at[slot, u] for u in range(n_side_pages)], side_ref)
    if with_ret:
        @pl.when(pl.program_id(1) % 2 == 0)
        def _():
            _ret_chunk(*ret_in, *ret_out, is_first=pl.program_id(2) == 0)

    qi = pl.program_id(2)
    blk = MOBA_BLOCK
    nb = seq // blk

    @pl.when(qi == 0)
    def _():
        km_ref[...] = jnp.mean(k_ref[...].reshape(nb, blk, A_HEAD_DIM), axis=1)
        vt_ref[...] = v_ref[...].T.astype(BF16)

    q_t = q_ref[...].T
    gate = jnp.dot(km_ref[...], q_t, precision=lax.Precision.HIGHEST,
                   preferred_element_type=F32)
    g = [jnp.where(n < qi, gate[n:n + 1, :], NEG) for n in range(nb)]
    scale = A_HEAD_DIM ** -0.5
    slope = slope_ref[...][:, :1]
    col_terms = []
    for n in range(nb):
        cnt = jnp.zeros((1, blk), F32)
        for m in range(nb):
            if m < n:
                cnt = cnt + jnp.where(g[m] >= g[n], 1.0, 0.0)
            elif m > n:
                cnt = cnt + jnp.where(g[m] > g[n], 1.0, 0.0)
        chosen = jnp.where(n < qi, jnp.where(cnt < MOBA_TOPK, 1.0, 0.0), jnp.where(n == qi, 1.0, 0.0))
        ahead = ((qi - n) * blk).astype(F32)
        col_terms.append(jnp.where(chosen > 0.5, -slope * ahead, NEG))

    rc = (lax.broadcasted_iota(I32, (blk, blk), 1) - lax.broadcasted_iota(I32, (blk, blk), 0))
    bias_past = -slope * rc.astype(F32)
    bias_own = jnp.where(rc >= 0, bias_past, NEG)
    q_tb = q_t.astype(BF16)

    def attend(n_keys_blocks):
        pieces = []
        for n in range(n_keys_blocks):
            kb = k_ref[n * blk:(n + 1) * blk, :].astype(BF16)
            s_t = jnp.dot(kb, q_tb, preferred_element_type=F32)
            pieces.append(s_t * scale + jnp.where(n == qi, bias_own, bias_past) + col_terms[n])
        logits = jnp.concatenate(pieces, axis=0)
        mx = jnp.max(logits, axis=0, keepdims=True)
        p = jnp.exp(logits - mx)
        denom = jnp.sum(p, axis=0, keepdims=True)
        out_t = jnp.dot(vt_ref[:, :n_keys_blocks * blk], p.astype(BF16), preferred_element_type=F32)
        o_ref[...] = (out_t / denom).T.astype(o_ref.dtype)

    half = nb // 2
    if half == 0:
        attend(nb)
    else:
        @pl.when(qi < half)
        def _():
            attend(half)

        @pl.when(qi >= half)
        def _():
            attend(nb)


def _side_pages_per_step(side_pool, side_page_table, n_steps):
    if side_pool is None:
        return 0
    bsz, n_pages = side_page_table.shape
    pages_per_block = MOBA_BLOCK // side_pool.shape[2]
    if (bsz * n_pages) % n_steps:
        return 0
    per_step = bsz * n_pages // n_steps
    if per_step % pages_per_block or n_pages % per_step or per_step > MAX_PAGES_PER_STEP:
        return 0
    return per_step


def _moba_prompt(qa, ka, va, slopes, batch, seq, side_pool=None, layer=0, side_page_table=None, ret=None):
    blk = MOBA_BLOCK
    nq = seq // blk
    n_side = _side_pages_per_step(side_pool, side_page_table, batch * A_HEADS * nq)
    in_specs = [pl.BlockSpec((None, 1, LANES), lambda b, h, i, pt: (h, 0, 0)),
                pl.BlockSpec((blk, A_HEAD_DIM), lambda b, h, i, pt: (b * nq + i, h)),
                pl.BlockSpec((seq, A_HEAD_DIM), lambda b, h, i, pt: (b, h)),
                pl.BlockSpec((seq, A_HEAD_DIM), lambda b, h, i, pt: (b, h))]
    out_specs = [pl.BlockSpec((blk, A_HEAD_DIM), lambda b, h, i, pt: (b * nq + i, h))]
    out_shape = [jax.ShapeDtypeStruct((batch * seq, A_WIDTH), BF16)]
    operands = [slopes, qa, ka, va]
    if n_side:
        page_size = side_pool.shape[2]
        side_bsz, n_pages = side_page_table.shape
        steps_per_seq = n_pages // n_side
        blocks_per_step = n_side // (MOBA_BLOCK // page_size)

        def flat(b, h, i):
            return (b * A_HEADS + h) * nq + i

        in_specs.append(pl.BlockSpec(memory_space=pl.ANY))
        operands.append(side_pool)
        out_specs.append(pl.BlockSpec(
            (None, blocks_per_step, A_HEADS, A_HEAD_DIM),
            lambda b, h, i, pt: (flat(b, h, i) // steps_per_seq, flat(b, h, i) % steps_per_seq, 0, 0)))
        out_shape.append(jax.ShapeDtypeStruct(
            (side_bsz, n_pages // (MOBA_BLOCK // page_size), A_HEADS, A_HEAD_DIM), F32))
        prefetch = side_page_table
        scratch = [pltpu.VMEM((RING_SLOTS, n_side, page_size, A_HEADS, A_HEAD_DIM), F32),
                   pltpu.SemaphoreType.DMA((RING_SLOTS,))]
        semantics = ("arbitrary", "arbitrary", "arbitrary")
    else:
        prefetch = jnp.zeros((1, 1), I32)
        scratch = []
        semantics = ("parallel", "parallel", "arbitrary")
    with_ret = bool(n_side) and ret is not None and A_HEADS == R_HEADS and seq // math.gcd(seq, RET_CHUNK) == nq
    if with_ret:
        def ret_index(b, h, i, pt):
            return (b, h // 2, jnp.where(h % 2 == 0, i, nq - 1))
        ret_operands, ret_in_specs, ret_out_specs, ret_out_shape, _ = _ret_chunk_operands(*ret, seq, ret_index)
        in_specs += ret_in_specs
        operands += ret_operands
        out_specs += ret_out_specs
        out_shape += ret_out_shape
    outs = pl.pallas_call(
        functools.partial(_moba_prompt_kernel, seq=seq, n_side_pages=n_side, layer=layer, with_ret=with_ret),
        grid_spec=pltpu.PrefetchScalarGridSpec(
            num_scalar_prefetch=1,
            grid=(batch, A_HEADS, nq),
            in_specs=in_specs,
            out_specs=out_specs,
            scratch_shapes=[pltpu.VMEM((nq, A_HEAD_DIM), F32),
                            pltpu.VMEM((A_HEAD_DIM, seq), BF16)]
            + scratch),
        out_shape=out_shape,
        compiler_params=_params(*semantics),
        name="moba_prompt",
    )(prefetch, *operands)
    return outs[0], (outs[1] if n_side else None), (tuple(outs[-2:]) if with_ret else None)


MAX_PAGES_PER_STEP = 16


def _page_mean_kernel(pt_ref, *refs):
    del pt_ref
    _block_means(refs[:-1], refs[-1])


def _page_means(pool, layer, page_table):
    page_size = pool.shape[2]
    bsz, n_pages = page_table.shape
    pages_per_block = MOBA_BLOCK // page_size
    pages_per_step = math.gcd(n_pages, MAX_PAGES_PER_STEP)
    assert pages_per_step % pages_per_block == 0
    blocks_per_step = pages_per_step // pages_per_block
    n_blk = n_pages // pages_per_block

    def page_spec(u):
        return pl.BlockSpec((None, None, page_size, A_HEADS, A_HEAD_DIM),
                            lambda b, g, pt: (layer, pt[b, g * pages_per_step + u], 0, 0, 0))

    return pl.pallas_call(
        _page_mean_kernel,
        grid_spec=pltpu.PrefetchScalarGridSpec(
            num_scalar_prefetch=1,
            grid=(bsz, n_pages // pages_per_step),
            in_specs=[page_spec(u) for u in range(pages_per_step)],
            out_specs=pl.BlockSpec((None, blocks_per_step, A_HEADS, A_HEAD_DIM),
                                   lambda b, g, pt: (b, g, 0, 0))),
        out_shape=jax.ShapeDtypeStruct((bsz, n_blk, A_HEADS, A_HEAD_DIM), F32),
        compiler_params=_params("parallel", "arbitrary"),
        name="page_means",
    )(page_table, *([pool] * pages_per_step))


def _block_choice_kernel(q_ref, km_ref, o_ref):
    rows = q_ref.shape[0]
    t_new = rows // A_HEADS
    n_blk = km_ref.shape[0]
    q = q_ref[...]
    r_head = lax.broadcasted_iota(I32, q.shape, 0) // t_new
    c_head = lax.broadcasted_iota(I32, q.shape, 1) // A_HEAD_DIM
    qbd = jnp.where(r_head == c_head, q, 0.0)
    gate = lax.dot_general(qbd, km_ref[...], NT_DIMS, precision=lax.Precision.HIGHEST,
                           preferred_element_type=F32)
    lane = lax.broadcasted_iota(I32, gate.shape, 1).astype(F32)
    out_lane = lax.broadcasted_iota(I32, (rows, LANES), 1)
    res = jnp.zeros((rows, LANES), F32)
    for it in range(MOBA_TOPK):
        mx = jnp.max(gate, axis=1, keepdims=True)
        idx = jnp.min(jnp.where(gate == mx, lane, float(n_blk)), axis=1, keepdims=True)
        res = jnp.where(out_lane == it, idx, res)
        gate = jnp.where(lane == idx, -jnp.inf, gate)
    o_ref[...] = res.astype(I32)


def _block_choice(q_rep, kmean):
    bsz, rows, _ = q_rep.shape
    n_blk = kmean.shape[1]
    return pl.pallas_call(
        _block_choice_kernel,
        grid=(bsz,),
        in_specs=[pl.BlockSpec((None, rows, A_WIDTH), lambda b: (b, 0, 0)),
                  pl.BlockSpec((None, n_blk, A_WIDTH), lambda b: (b, 0, 0))],
        out_specs=pl.BlockSpec((None, rows, LANES), lambda b: (b, 0, 0)),
        out_shape=jax.ShapeDtypeStruct((bsz, rows, LANES), I32),
        compiler_params=_params("parallel"),
        name="block_choice",
    )(q_rep, kmean)


def _moba_paged_kernel(pt_ref, blk_ref, slope_ref, q_ref, kn_ref, vn_ref, pk_ref, pv_ref, o_ref,
                       kbuf, vbuf, sem, *, layer, t_new, page_size, past_len):
    pages_per_block = MOBA_BLOCK // page_size
    tiles_per_q = MOBA_TOPK * pages_per_block
    n_heads = pl.num_programs(1)
    step = pl.program_id(0) * n_heads + pl.program_id(1)
    n_steps = pl.num_programs(0) * n_heads

    def chosen_block(step_i, t, j):
        return blk_ref[(step_i * t_new + t) * MOBA_TOPK + j]

    def tile_copies(step_i, slot_i):
        b = step_i // n_heads
        h = step_i % n_heads
        copies = []
        for t in range(t_new):
            for j in range(MOBA_TOPK):
                block = chosen_block(step_i, t, j)
                for half in range(pages_per_block):
                    page = pt_ref[b, block * pages_per_block + half]
                    idx = t * tiles_per_q + j * pages_per_block + half
                    copies.append(pltpu.make_async_copy(pk_ref.at[layer, page, :, h, :],
                                                        kbuf.at[slot_i, idx], sem.at[slot_i]))
                    copies.append(pltpu.make_async_copy(pv_ref.at[layer, page, :, h, :],
                                                        vbuf.at[slot_i, idx], sem.at[slot_i]))
        return copies

    slot = _ring_prefetch(step, n_steps, kbuf.shape[0], tile_copies)

    scale = A_HEAD_DIM ** -0.5
    slope = slope_ref[...][:, :1]
    row = lax.broadcasted_iota(I32, (page_size, 1), 0)
    row_new = lax.broadcasted_iota(I32, (t_new, 1), 0)
    kn = kn_ref[...]
    vn = vn_ref[...]
    for t in range(t_new):
        qt = q_ref[t:t + 1, :]
        cols = []
        for j in range(MOBA_TOPK):
            block = chosen_block(step, t, j)
            for half in range(pages_per_block):
                kt = kbuf[slot, t * tiles_per_q + j * pages_per_block + half]
                lg = jnp.sum(kt * qt, axis=-1, keepdims=True) * scale
                dist = (past_len + t) - (block * MOBA_BLOCK + half * page_size + row)
                cols.append(lg - slope * dist.astype(F32))
        dist_new = t - row_new
        lg_new = jnp.sum(kn * qt, axis=-1, keepdims=True) * scale - slope * dist_new.astype(F32)
        lg_new = jnp.where(dist_new >= 0, lg_new, NEG)
        mx = jnp.max(lg_new, axis=0, keepdims=True)
        for c in cols:
            mx = jnp.maximum(mx, jnp.max(c, axis=0, keepdims=True))
        p_new = jnp.exp(lg_new - mx)
        denom = jnp.sum(p_new, axis=0, keepdims=True)
        acc = jnp.sum(p_new * vn, axis=0, keepdims=True)
        for idx, c in enumerate(cols):
            p = jnp.exp(c - mx)
            denom = denom + jnp.sum(p, axis=0, keepdims=True)
            acc = acc + jnp.sum(p * vbuf[slot, t * tiles_per_q + idx], axis=0, keepdims=True)
        o_ref[t:t + 1, :] = acc / denom


def _moba_paged(qa, ka, va, pool_k, pool_v, layer, page_table, blocks, slopes):
    bsz, t_new, _ = qa.shape
    page_size = pool_k.shape[2]
    n_pages = page_table.shape[1]
    past_len = n_pages * page_size
    n_tiles = t_new * MOBA_TOPK * (MOBA_BLOCK // page_size)
    new_spec = pl.BlockSpec((None, t_new, A_HEAD_DIM), lambda b, h, pt, blk: (b, 0, h))
    hbm_spec = pl.BlockSpec(memory_space=pl.ANY)
    return pl.pallas_call(
        functools.partial(_moba_paged_kernel, layer=layer, t_new=t_new, page_size=page_size,
                          past_len=past_len),
        grid_spec=pltpu.PrefetchScalarGridSpec(
            num_scalar_prefetch=2,
            grid=(bsz, A_HEADS),
            in_specs=[pl.BlockSpec((None, 1, LANES), lambda b, h, pt, blk: (h, 0, 0)),
                      new_spec, new_spec, new_spec, hbm_spec, hbm_spec],
            out_specs=new_spec,
            scratch_shapes=[pltpu.VMEM((RING_SLOTS, n_tiles, page_size, A_HEAD_DIM), F32),
                            pltpu.VMEM((RING_SLOTS, n_tiles, page_size, A_HEAD_DIM), F32),
                            pltpu.SemaphoreType.DMA((RING_SLOTS,))]),
        out_shape=jax.ShapeDtypeStruct((bsz, t_new, A_WIDTH), F32),
        compiler_params=_params("arbitrary", "arbitrary"),
        name="moba_paged",
    )(page_table, blocks, slopes, qa, ka, va, pool_k, pool_v)


def _ret_chunk_kernel(*refs):
    _ret_chunk(*refs, is_first=pl.program_id(2) == 0)


def _ret_tables(chunk):
    log_gamma = jnp.log1p(-jnp.exp2(-5.0 - jnp.arange(R_HEADS, dtype=F32)))
    idx = jnp.arange(chunk, dtype=F32)
    diff = idx[:, None] - idx[None, :]
    causal = diff >= 0
    decay_in = jnp.where(causal[None], jnp.exp(jnp.where(causal, diff, 0.0)[None] * log_gamma[:, None, None]), 0.0)
    q_dec = jnp.exp((idx + 1)[None, :] * log_gamma[:, None])
    k_dec = jnp.exp((chunk - 1 - idx)[None, :] * log_gamma[:, None])
    chunk_dec = jnp.exp(chunk * log_gamma)
    return decay_in, q_dec, k_dec, chunk_dec


def _pair_lanes(k_dec):
    h, c = k_dec.shape
    t = jnp.broadcast_to(k_dec.reshape(h // 2, 2, c, 1), (h // 2, 2, c, R_QK_DIM))
    return t.transpose(0, 2, 1, 3).reshape(h // 2, c, 2 * R_QK_DIM)


def _ret_chunk_operands(r, s0, ret_norm_w, seq, index):
    batch = s0.shape[0]
    c = math.gcd(seq, RET_CHUNK)
    nc = seq // c
    decay_in, q_dec, k_dec, chunk_dec = _ret_tables(c)
    qd = q_dec[:, :, None]
    kd = _pair_lanes(k_dec)
    cd = jnp.broadcast_to(chunk_dec[:, None, None], (R_HEADS, 1, LANES))
    qk_blocks = R_QK_WIDTH // LANES
    v_off = 2 * R_QK_WIDTH // (2 * R_V_DIM)
    g_off = (2 * R_QK_WIDTH + R_WIDTH) // (2 * R_V_DIM)

    def rows(col0):
        def imap(*ids):
            b, p, i = index(*ids)
            return (b * nc + i, col0 + p)
        return imap

    def pair(*tail):
        def imap(*ids):
            return (index(*ids)[1],) + tail
        return imap

    def state(*ids):
        b, p, _ = index(*ids)
        return (b, p, 0, 0)

    in_specs = [pl.BlockSpec((c, LANES), rows(0)),
                pl.BlockSpec((c, LANES), rows(qk_blocks)),
                pl.BlockSpec((c, 2 * R_V_DIM), rows(v_off)),
                pl.BlockSpec((c, 2 * R_V_DIM), rows(g_off)),
                pl.BlockSpec((2, c, c), pair(0, 0)),
                pl.BlockSpec((2, c, 1), pair(0, 0)),
                pl.BlockSpec((None, c, LANES), pair(0, 0)),
                pl.BlockSpec((2, 1, LANES), pair(0, 0)),
                pl.BlockSpec((None, 2, R_QK_DIM, R_V_DIM), state),
                pl.BlockSpec((1, R_V_DIM), lambda *ids: (0, 0))]
    out_specs = [pl.BlockSpec((c, 2 * R_V_DIM), rows(0)),
                 pl.BlockSpec((None, 2, R_QK_DIM, R_V_DIM), state)]
    out_shape = [jax.ShapeDtypeStruct((batch * seq, R_WIDTH), BF16),
                 jax.ShapeDtypeStruct((batch, R_HEADS, R_QK_DIM, R_V_DIM), F32)]
    operands = [r, r, r, r, decay_in, qd, kd, cd, s0, ret_norm_w]
    assert len(operands) == N_RET_INPUTS
    return operands, in_specs, out_specs, out_shape, nc


def _retention_prompt(r, s0, ret_norm_w, batch, seq):
    operands, in_specs, out_specs, out_shape, nc = _ret_chunk_operands(
        r, s0, ret_norm_w, seq, lambda b, p, i: (b, p, i))
    return pl.pallas_call(
        _ret_chunk_kernel,
        grid=(batch, R_HEADS // 2, nc),
        in_specs=in_specs,
        out_specs=out_specs,
        out_shape=out_shape,
        compiler_params=_params("parallel", "parallel", "arbitrary"),
        name="retention_chunks",
    )(*operands)


def _ret_step_kernel(q_ref, k_ref, v_ref, g_ref, din_ref, qd_ref, kd_ref, cd_ref, s0_ref, nw_ref,
                     o_ref, s_ref, *, t_new):
    rows = q_ref.shape[0]
    bsz = rows // t_new
    k_t = (k_ref[...] * kd_ref[...]).T
    seq_of_row = lax.broadcasted_iota(I32, (rows, bsz * R_QK_DIM), 0) // t_new
    seq_of_col = lax.broadcasted_iota(I32, (rows, bsz * R_QK_DIM), 1) // R_QK_DIM
    own_q = seq_of_row == seq_of_col
    seq_of_srow = lax.broadcasted_iota(I32, (bsz * R_QK_DIM, rows), 0) // R_QK_DIM
    seq_of_scol = lax.broadcasted_iota(I32, (bsz * R_QK_DIM, rows), 1) // t_new
    own_k = seq_of_srow == seq_of_scol
    for hh in range(2):
        q = q_ref[:, hh * R_QK_DIM:(hh + 1) * R_QK_DIM]
        qb = q.astype(BF16)
        kb = k_ref[:, hh * R_QK_DIM:(hh + 1) * R_QK_DIM].astype(BF16)
        vb = v_ref[:, hh * R_V_DIM:(hh + 1) * R_V_DIM].astype(BF16)
        s = s0_ref[:, hh].reshape(bsz * R_QK_DIM, R_V_DIM)
        inner = lax.dot_general(qb, kb, NT_DIMS, preferred_element_type=F32) * din_ref[hh]
        q_wide = jnp.where(own_q, jnp.concatenate([q] * bsz, axis=1), 0.0).astype(BF16)
        o = (jnp.dot(inner.astype(BF16), vb, preferred_element_type=F32)
             + jnp.dot(q_wide, s.astype(BF16), preferred_element_type=F32) * qd_ref[hh])
        kdt = k_t[hh * R_QK_DIM:(hh + 1) * R_QK_DIM, :]
        k_tall = jnp.where(own_k, jnp.concatenate([kdt] * bsz, axis=0), 0.0).astype(BF16)
        s_new = s * cd_ref[hh] + jnp.dot(k_tall, vb, preferred_element_type=F32)
        s_ref[:, hh] = s_new.reshape(bsz, R_QK_DIM, R_V_DIM)
        o_ref[:, hh * R_V_DIM:(hh + 1) * R_V_DIM] = _ret_epilogue(
            o, g_ref[:, hh * R_V_DIM:(hh + 1) * R_V_DIM], nw_ref[...])


def _retention_step(r, s0, ret_norm_w, batch, t_new):
    rows = batch * t_new
    decay_in, q_dec, k_dec, chunk_dec = _ret_tables(t_new)
    same_seq = jnp.eye(batch, dtype=F32)
    din = jnp.einsum("hij,ab->haibj", decay_in, same_seq).reshape(R_HEADS, rows, rows)
    qd = jnp.tile(q_dec, (1, batch))[:, :, None]
    kd = _pair_lanes(jnp.tile(k_dec, (1, batch)))
    cd = jnp.broadcast_to(chunk_dec[:, None, None], (R_HEADS, 1, LANES))
    qk_blocks = R_QK_WIDTH // LANES
    v_off = 2 * R_QK_WIDTH // (2 * R_V_DIM)
    g_off = (2 * R_QK_WIDTH + R_WIDTH) // (2 * R_V_DIM)
    return pl.pallas_call(
        functools.partial(_ret_step_kernel, t_new=t_new),
        grid=(R_HEADS // 2,),
        in_specs=[pl.BlockSpec((rows, LANES), lambda p: (0, p)),
                  pl.BlockSpec((rows, LANES), lambda p: (0, qk_blocks + p)),
                  pl.BlockSpec((rows, 2 * R_V_DIM), lambda p: (0, v_off + p)),
                  pl.BlockSpec((rows, 2 * R_V_DIM), lambda p: (0, g_off + p)),
                  pl.BlockSpec((2, rows, rows), lambda p: (p, 0, 0)),
                  pl.BlockSpec((2, rows, 1), lambda p: (p, 0, 0)),
                  pl.BlockSpec((None, rows, LANES), lambda p: (p, 0, 0)),
                  pl.BlockSpec((2, 1, LANES), lambda p: (p, 0, 0)),
                  pl.BlockSpec((batch, 2, R_QK_DIM, R_V_DIM), lambda p: (0, p, 0, 0)),
                  pl.BlockSpec((1, R_V_DIM), lambda p: (0, 0))],
        out_specs=[pl.BlockSpec((rows, 2 * R_V_DIM), lambda p: (0, p)),
                   pl.BlockSpec((batch, 2, R_QK_DIM, R_V_DIM), lambda p: (0, p, 0, 0))],
        out_shape=[jax.ShapeDtypeStruct((rows, R_WIDTH), BF16),
                   jax.ShapeDtypeStruct((batch, R_HEADS, R_QK_DIM, R_V_DIM), F32)],
        compiler_params=_params("parallel"),
        name="retention_step",
    )(r, r, r, r, din, qd, kd, cd, s0, ret_norm_w)


def _outproj_kernel(a_ref, r_ref, x_ref, g1_ref, sh_ref, sc_ref, nw_ref, wa_ref, wr_ref,
                    x1_ref, h2_ref):
    mixed = (jnp.dot(a_ref[...].astype(BF16), wa_ref[...], preferred_element_type=F32)
             + jnp.dot(r_ref[...].astype(BF16), wr_ref[...], preferred_element_type=F32))
    x1 = x_ref[...] + g1_ref[...] * mixed
    x1_ref[...] = x1
    var = jnp.mean(x1 * x1, axis=-1, keepdims=True)
    h2 = x1 * lax.rsqrt(var + EPS) * nw_ref[...]
    h2_ref[...] = (h2 * (1 + sc_ref[...]) + sh_ref[...]).astype(BF16)


def _outproj(out_a, out_r, x, gate, shift, scale, norm_w, w_out, tm, tiles_per_mod):
    m, d = x.shape
    mod_rows = shift.shape[1]
    mod_spec = pl.BlockSpec((None, mod_rows, d), lambda i: (i // tiles_per_mod, 0, 0))
    return pl.pallas_call(
        _outproj_kernel,
        grid=(m // tm,),
        in_specs=[pl.BlockSpec((tm, A_WIDTH), lambda i: (i, 0)),
                  pl.BlockSpec((tm, R_WIDTH), lambda i: (i, 0)),
                  pl.BlockSpec((tm, d), lambda i: (i, 0)),
                  mod_spec, mod_spec, mod_spec,
                  pl.BlockSpec((1, d), lambda i: (0, 0)),
                  pl.BlockSpec((A_WIDTH, d), lambda i: (0, 0)),
                  pl.BlockSpec((R_WIDTH, d), lambda i: (A_WIDTH // R_WIDTH, 0))],
        out_specs=[pl.BlockSpec((tm, d), lambda i: (i, 0)),
                   pl.BlockSpec((tm, d), lambda i: (i, 0))],
        out_shape=[jax.ShapeDtypeStruct((m, d), F32), jax.ShapeDtypeStruct((m, d), BF16)],
        compiler_params=_params("parallel"),
        name="outproj",
    )(out_a, out_r, x, gate, shift, scale, norm_w, w_out, w_out)


def _pruned_pairs():
    return [(a, P_TOPK // (a + 1)) for a in range(P_TOPK)]


N_CAND = sum(nb for _, nb in _pruned_pairs())
CAND_ROWS = -(-N_CAND // 8) * 8


def _top_rows(x, k, val_ref, idx_ref):
    nrows = x.shape[0]
    row = lax.broadcasted_iota(I32, x.shape, 0).astype(F32)
    for i in range(k):
        mx = jnp.max(x, axis=0, keepdims=True)
        am = jnp.min(jnp.where(x == mx, row, float(nrows)), axis=0, keepdims=True)
        val_ref[i:i + 1, :] = mx
        idx_ref[i:i + 1, :] = am
        x = jnp.where(row == am, -jnp.inf, x)


def _peer_route_kernel(h_ref, wq_ref, sk_ref, i1_ref, i2_ref, g_ref,
                       s1_ref, k1_ref, s2_ref, k2_ref, cand_ref, c1_ref, c2_ref):
    tm = h_ref.shape[0]
    qp = jnp.dot(h_ref[...], wq_ref[...], preferred_element_type=F32)
    cand_ref[...] = jnp.full(cand_ref.shape, -jnp.inf, F32)
    c1_ref[...] = jnp.zeros(c1_ref.shape, F32)
    c2_ref[...] = jnp.zeros(c2_ref.shape, F32)
    row = lax.broadcasted_iota(I32, (CAND_ROWS, tm), 0).astype(F32)
    for hd in range(P_HEADS):
        for half, (s_ref, k_ref) in enumerate(((s1_ref, k1_ref), (s2_ref, k2_ref))):
            col = (hd * 2 + half) * LANES
            q_part = qp[:, col:col + LANES].astype(BF16)
            scores = lax.dot_general(sk_ref[hd * 2 + half], q_part, NT_DIMS,
                                     preferred_element_type=F32)
            _top_rows(scores, P_TOPK, s_ref, k_ref)
        off = 0
        for a, nb in _pruned_pairs():
            cand_ref[off:off + nb, :] = s1_ref[a:a + 1, :] + s2_ref[0:nb, :]
            c1_ref[off:off + nb, :] = jnp.broadcast_to(k1_ref[a:a + 1, :], (nb, tm))
            c2_ref[off:off + nb, :] = k2_ref[0:nb, :]
            off += nb
        cand = cand_ref[...]
        c1 = c1_ref[...]
        c2 = c2_ref[...]
        best, e1, e2 = [], [], []
        for _ in range(P_TOPK):
            mx = jnp.max(cand, axis=0, keepdims=True)
            am = jnp.min(jnp.where(cand == mx, row, float(CAND_ROWS)), axis=0, keepdims=True)
            hit = row == am
            best.append(mx)
            e1.append(jnp.max(jnp.where(hit, c1, -1.0), axis=0, keepdims=True))
            e2.append(jnp.max(jnp.where(hit, c2, -1.0), axis=0, keepdims=True))
            cand = jnp.where(hit, -jnp.inf, cand)
        top = best[0]
        exps = [jnp.exp(v - top) for v in best]
        denom = exps[0]
        for v in exps[1:]:
            denom = denom + v
        for i in range(P_TOPK):
            slot = hd * P_TOPK + i
            i1_ref[slot:slot + 1, :] = e1[i].astype(I32)
            i2_ref[slot:slot + 1, :] = e2[i].astype(I32)
            g_ref[slot:slot + 1, :] = exps[i] / denom


def _peer_route(h2, w_query, sub_keys, tm):
    m, d = h2.shape
    qd = w_query.shape[1]
    slot_spec = pl.BlockSpec((P_SLOTS, tm), lambda i: (0, i))
    return pl.pallas_call(
        _peer_route_kernel,
        grid=(m // tm,),
        in_specs=[pl.BlockSpec((tm, d), lambda i: (i, 0)),
                  pl.BlockSpec((d, qd), lambda i: (0, 0)),
                  pl.BlockSpec(sub_keys.shape, lambda i: (0, 0, 0))],
        out_specs=[slot_spec, slot_spec, slot_spec],
        out_shape=[jax.ShapeDtypeStruct((P_SLOTS, m), I32),
                   jax.ShapeDtypeStruct((P_SLOTS, m), I32),
                   jax.ShapeDtypeStruct((P_SLOTS, m), F32)],
        scratch_shapes=[pltpu.VMEM((P_TOPK, tm), F32)] * 4 + [pltpu.VMEM((CAND_ROWS, tm), F32)] * 3,
        compiler_params=_params("parallel"),
        name="peer_route",
    )(h2, w_query, sub_keys)


SUBLANES = 8


def _peer_mask_kernel(i1_ref, i2_ref, g_ref, o_ref):
    key = lax.broadcasted_iota(I32, (P_NKEYS, P_SLOTS), 0)

    def body(grp, carry):
        base = pl.multiple_of(grp * SUBLANES, SUBLANES)
        mats = []
        for u in range(SUBLANES):
            i1 = i1_ref[pl.ds(base + u, 1), :]
            i2 = i2_ref[pl.ds(base + u, 1), :]
            g = g_ref[pl.ds(base + u, 1), :]
            g_hi = g.astype(BF16).astype(F32)
            g_lo = g - g_hi
            hit1 = key == i1
            a = jnp.concatenate([jnp.where(hit1, g_hi, 0.0).astype(BF16),
                                 jnp.where(hit1, g_lo, 0.0).astype(BF16)], axis=1)
            onehot2 = jnp.where(key == i2, 1.0, 0.0).astype(BF16)
            bm = jnp.concatenate([onehot2, onehot2], axis=1)
            mats.append(lax.dot_general(a, bm, NT_DIMS, preferred_element_type=F32))
        o_ref[grp] = jnp.swapaxes(jnp.stack(mats, axis=0), 0, 1)
        return carry

    lax.fori_loop(0, o_ref.shape[0], body, 0, unroll=4)


def _peer_mask(i1, i2, gates, tt):
    m = i1.shape[0]
    slot_spec = pl.BlockSpec((tt, P_SLOTS), lambda i: (i, 0))
    return pl.pallas_call(
        _peer_mask_kernel,
        grid=(m // tt,),
        in_specs=[slot_spec, slot_spec, slot_spec],
        out_specs=pl.BlockSpec((tt // SUBLANES, P_NKEYS, SUBLANES, P_NKEYS), lambda i: (i, 0, 0, 0)),
        out_shape=jax.ShapeDtypeStruct((m // SUBLANES, P_NKEYS, SUBLANES, P_NKEYS), F32),
        compiler_params=_params("parallel"),
        name="peer_mask",
    )(i1, i2, gates)


def _peer_mlp_kernel(h_ref, dn_ref, up_ref, m_ref, x_ref, g2_ref, o_ref, acc_ref):
    e = pl.program_id(1)

    @pl.when(e == 0)
    def _():
        acc_ref[...] = jnp.zeros(acc_ref.shape, F32)

    pre = lax.dot_general(h_ref[...], dn_ref[...], NT_DIMS, preferred_element_type=F32)
    act = 0.5 * pre * (1.0 + lax.erf(pre * (0.5 ** 0.5)))
    tm = act.shape[0]
    parts = []
    for a in range(m_ref.shape[1]):
        gate = m_ref[:, a].reshape(tm, P_NKEYS)
        parts.append((act[:, a * P_NKEYS:(a + 1) * P_NKEYS] * gate).astype(BF16))
    w = jnp.concatenate(parts, axis=1)
    acc_ref[...] += jnp.dot(w, up_ref[...], preferred_element_type=F32)

    @pl.when(e == pl.num_programs(1) - 1)
    def _():
        o_ref[...] = x_ref[...] + g2_ref[...] * acc_ref[...]


def _peer_mlp(h2, down, up, mask, x1, gate, tm, te, tiles_per_mod):
    m, d = h2.shape
    n_exp = down.shape[0]
    mod_rows = gate.shape[1]
    assert te % P_NKEYS == 0 and tm % SUBLANES == 0
    return pl.pallas_call(
        _peer_mlp_kernel,
        grid=(m // tm, n_exp // te),
        in_specs=[pl.BlockSpec((tm, d), lambda i, e: (i, 0), pipeline_mode=pl.Buffered(1)),
                  pl.BlockSpec((te, d), lambda i, e: (e, 0)),
                  pl.BlockSpec((te, d), lambda i, e: (e, 0)),
                  pl.BlockSpec((tm // SUBLANES, te // P_NKEYS, SUBLANES, P_NKEYS),
                               lambda i, e: (i, e, 0, 0)),
                  pl.BlockSpec((tm, d), lambda i, e: (i, 0), pipeline_mode=pl.Buffered(1)),
                  pl.BlockSpec((None, mod_rows, d), lambda i, e: (i // tiles_per_mod, 0, 0))],
        out_specs=pl.BlockSpec((tm, d), lambda i, e: (i, 0)),
        out_shape=jax.ShapeDtypeStruct((m, d), F32),
        scratch_shapes=[pltpu.VMEM((tm, d), F32)],
        compiler_params=_params("parallel", "arbitrary"),
        name="peer_mlp",
    )(h2, down, up, mask, x1, gate)


def _alibi_slope_rows():
    slopes = jnp.exp2(-8.0 * jnp.arange(1, A_HEADS + 1, dtype=F32) / A_HEADS)
    return jnp.broadcast_to(slopes[:, None, None], (A_HEADS, 1, LANES))


def _layer(x, mod, pool_k, pool_v, layer, page_table, s0, w, *, tm, tm_mlp, side=None, kmean=None):
    bsz, seq, d = x.shape
    m = bsz * seq
    xf = x.reshape(m, d)
    chunks = jnp.split(mod, 6, axis=-1)
    if seq % tm == 0:
        mods = [c[:, None, :] for c in chunks]
        tiles_per_mod = seq // tm
        tiles_per_mod_mlp = seq // tm_mlp
    else:
        assert m == tm == tm_mlp
        mods = [jnp.repeat(c, seq, axis=0)[None] for c in chunks]
        tiles_per_mod = tiles_per_mod_mlp = 1
    sh1, sc1, g1, sh2, sc2, g2 = mods

    proj = functools.partial(_inproj, xf, w["norm1_w"], sh1, sc1, w["w_in"], tm=tm,
                             tiles_per_mod=tiles_per_mod)
    qa = proj(col0=0, ncols=A_WIDTH, aux=w["q_norm_w"], mode="headnorm")
    ka = proj(col0=A_WIDTH, ncols=A_WIDTH, aux=w["k_norm_w"], mode="headnorm")
    va = proj(col0=2 * A_WIDTH, ncols=A_WIDTH, aux=w["ones_a"], mode="scale")
    r = proj(col0=3 * A_WIDTH, ncols=2 * R_QK_WIDTH + 2 * R_WIDTH, aux=w["r_scale"], mode="scale")

    slopes = _alibi_slope_rows()
    if pool_k is None:
        side_pool, side_table = side if side is not None else (None, None)
        out_a, side_kmean, fused_ret = _moba_prompt(qa, ka, va, slopes, bsz, seq, side_pool, layer, side_table,
                                                    ret=(r, s0, w["ret_norm_w"]))
        out_r, s_new = fused_ret if fused_ret is not None else _retention_prompt(r, s0, w["ret_norm_w"], bsz, seq)
    else:
        side_kmean = None
        if kmean is None:
            kmean = _page_means(pool_k, layer, page_table)
        q3 = qa.reshape(bsz, seq, A_WIDTH)
        q_rep = jnp.broadcast_to(q3[:, None], (bsz, A_HEADS, seq, A_WIDTH)).reshape(bsz, A_HEADS * seq, A_WIDTH)
        choice = _block_choice(q_rep, kmean.reshape(bsz, kmean.shape[1], A_WIDTH))
        blocks = choice[:, :, :MOBA_TOPK].reshape(-1)
        out_a = _moba_paged(q3, ka.reshape(bsz, seq, A_WIDTH), va.reshape(bsz, seq, A_WIDTH),
                            pool_k, pool_v, layer, page_table, blocks, slopes).reshape(m, A_WIDTH)
        out_r, s_new = _retention_step(r, s0, w["ret_norm_w"], bsz, seq)

    tm_small = min(tm, 256)
    x1, h2 = _outproj(out_a, out_r, xf, g1, sh2, sc2, w["norm2_w"], w["w_out"], tm_small,
                      tiles_per_mod * (tm // tm_small))

    i1, i2, gates = _peer_route(h2, w["peer_w_query"], w["peer_sub_keys"], tm=tm_small)
    mask = _peer_mask(i1.T, i2.T, gates.T, tt=min(m, 64))
    y = _peer_mlp(h2, w["peer_down"], w["peer_up"], mask, x1, g2,
                  tm=tm_mlp, te=1024, tiles_per_mod=tiles_per_mod_mlp)
    k_out = ka.reshape(bsz, seq, A_HEADS, A_HEAD_DIM)
    v_out = va.reshape(bsz, seq, A_HEADS, A_HEAD_DIM)
    return y.reshape(bsz, seq, d), k_out, v_out, s_new, side_kmean


def kernel(x_prompt, x_sample, cache_k, cache_v, state_ret, page_table, c_prompt, c_sample, w_ada, b_ada, norm1_w, w_in, q_norm_w, k_norm_w, ret_norm_w, w_out, norm2_w, peer_w_query, peer_sub_keys, peer_down, peer_up):
    depth = w_ada.shape[0]
    n_prompt = c_prompt.shape[0]
    n_sample = c_sample.shape[0]
    pad = (-(n_prompt + n_sample)) % 8
    d = x_prompt.shape[-1]
    hp, hs = x_prompt, x_sample
    s0_prompt = jnp.zeros((n_prompt, R_HEADS, R_QK_DIM, R_V_DIM), F32)
    r_scale = jnp.concatenate([jnp.ones((1, R_QK_WIDTH), F32),
                               jnp.full((1, R_QK_WIDTH), R_QK_DIM ** -0.5, F32),
                               jnp.ones((1, 2 * R_WIDTH), F32)], axis=1)
    outs = [[] for _ in range(6)]
    for l in range(depth):
        c_all = jnp.concatenate([c_prompt, c_sample, jnp.zeros((pad, d), F32)], axis=0)
        mod = _ada(c_all, w_ada[l], b_ada[l][None])
        w = dict(norm1_w=norm1_w[l][None], w_in=w_in[l].astype(BF16), q_norm_w=q_norm_w[l][None],
                 k_norm_w=k_norm_w[l][None], ret_norm_w=ret_norm_w[l][None],
                 w_out=w_out[l].astype(BF16), norm2_w=norm2_w[l][None],
                 peer_w_query=peer_w_query[l].astype(BF16),
                 peer_sub_keys=peer_sub_keys[l].reshape(2 * P_HEADS, P_NKEYS, -1).astype(BF16),
                 peer_down=peer_down[l].astype(BF16), peer_up=peer_up[l].astype(BF16),
                 ones_a=jnp.ones((1, A_WIDTH), F32), r_scale=r_scale)
        hp, k1, v1, s1, kmean = _layer(hp, mod[:n_prompt], None, None, l, None, s0_prompt, w, tm=512,
                                       tm_mlp=512, side=(cache_k, page_table))
        n_new = hs.shape[0] * hs.shape[1]
        hs, k2, v2, s2, _ = _layer(hs, mod[n_prompt:n_prompt + n_sample], cache_k, cache_v, l, page_table,
                                   state_ret[l], w, kmean=kmean, tm=n_new, tm_mlp=n_new)
        for lst, val in zip(outs, (k1, v1, s1, k2, v2, s2)):
            lst.append(val)
    return (hp, hs) + tuple(jnp.stack(o) for o in outs)
```

```python
import functools
import math

import jax
import jax.numpy as jnp
from jax import lax
from jax.experimental import pallas as pl
from jax.experimental.pallas import tpu as pltpu

F32 = jnp.float32
BF16 = jnp.bfloat16
I32 = jnp.int32

EPS = 1e-6
NEG = -1e30

A_HEADS = 8
A_HEAD_DIM = 128
A_WIDTH = A_HEADS * A_HEAD_DIM
MOBA_BLOCK = 256
MOBA_TOPK = 3
R_HEADS = 8
R_QK_DIM = 64
R_V_DIM = 128
R_QK_WIDTH = R_HEADS * R_QK_DIM
R_WIDTH = R_HEADS * R_V_DIM
RET_CHUNK = 256
P_HEADS = 8
P_NKEYS = 128
P_TOPK = 16
P_SLOTS = P_HEADS * P_TOPK

LANES = 128
VMEM_LIMIT_BYTES = 48 * 1024 * 1024

NT_DIMS = (((1,), (1,)), ((), ()))


def _params(*semantics):
    return pltpu.CompilerParams(dimension_semantics=semantics, vmem_limit_bytes=VMEM_LIMIT_BYTES)


def _silu(x):
    return x * jax.nn.sigmoid(x)


def _ada_kernel(c_ref, w_ref, b_ref, o_ref):
    s = _silu(c_ref[...]).astype(BF16)
    o_ref[...] = jnp.dot(s, w_ref[...].astype(BF16), preferred_element_type=F32) + b_ref[...]


def _ada(c, w, b):
    rows, d = c.shape
    n = w.shape[1]
    tn = 1024
    return pl.pallas_call(
        _ada_kernel,
        grid=(n // tn,),
        in_specs=[pl.BlockSpec((rows, d), lambda j: (0, 0)),
                  pl.BlockSpec((d, tn), lambda j: (0, j)),
                  pl.BlockSpec((1, tn), lambda j: (0, j))],
        out_specs=pl.BlockSpec((rows, tn), lambda j: (0, j)),
        out_shape=jax.ShapeDtypeStruct((rows, n), F32),
        compiler_params=_params("parallel"),
        name="ada",
    )(c, w, b)


def _inproj_kernel(x_ref, nw_ref, sh_ref, sc_ref, w_ref, aux_ref, o_ref, *, mode):
    x = x_ref[...]
    var = jnp.mean(x * x, axis=-1, keepdims=True)
    h = x * lax.rsqrt(var + EPS) * nw_ref[...]
    h = h * (1 + sc_ref[...]) + sh_ref[...]
    y = jnp.dot(h.astype(BF16), w_ref[...], preferred_element_type=F32)
    if mode == "headnorm":
        for hd in range(y.shape[1] // A_HEAD_DIM):
            yh = y[:, hd * A_HEAD_DIM:(hd + 1) * A_HEAD_DIM]
            v = jnp.mean(yh * yh, axis=-1, keepdims=True)
            o_ref[:, hd * A_HEAD_DIM:(hd + 1) * A_HEAD_DIM] = yh * lax.rsqrt(v + EPS) * aux_ref[...]
    elif mode == "scale":
        o_ref[...] = y * aux_ref[...]
    else:
        o_ref[...] = y


def _inproj(x, norm_w, shift, scale, w, col0, ncols, aux, mode, tm, tiles_per_mod):
    m, d = x.shape
    tn = 1024
    assert ncols % tn == 0 and col0 % tn == 0 and m % tm == 0
    mod_rows = shift.shape[1]
    mod_spec = pl.BlockSpec((None, mod_rows, d), lambda i, j: (i // tiles_per_mod, 0, 0))
    if mode == "headnorm":
        aux_spec = pl.BlockSpec((1, A_HEAD_DIM), lambda i, j: (0, 0))
    else:
        aux_spec = pl.BlockSpec((1, tn), lambda i, j: (0, j))
    return pl.pallas_call(
        functools.partial(_inproj_kernel, mode=mode),
        grid=(m // tm, ncols // tn),
        in_specs=[pl.BlockSpec((tm, d), lambda i, j: (i, 0)),
                  pl.BlockSpec((1, d), lambda i, j: (0, 0)),
                  mod_spec, mod_spec,
                  pl.BlockSpec((d, tn), lambda i, j: (0, col0 // tn + j)),
                  aux_spec],
        out_specs=pl.BlockSpec((tm, tn), lambda i, j: (i, j)),
        out_shape=jax.ShapeDtypeStruct((m, ncols), F32),
        compiler_params=_params("parallel", "arbitrary"),
        name="inproj_" + mode,
    )(x, norm_w, shift, scale, w, aux)


RING_SLOTS = 3


def _ring_prefetch(step, n_steps, n_slots, copies_of):
    slot = step % n_slots

    @pl.when(step == 0)
    def _():
        for ahead in range(n_slots - 1):
            @pl.when(ahead < n_steps)
            def _(ahead=ahead):
                for c in copies_of(ahead, ahead):
                    c.start()

    @pl.when(step + (n_slots - 1) < n_steps)
    def _():
        for c in copies_of(step + (n_slots - 1), (step + (n_slots - 1)) % n_slots):
            c.start()

    for c in copies_of(step, slot):
        c.wait()
    return slot


def _ret_epilogue(o, g, nw):
    var = jnp.mean(o * o, axis=-1, keepdims=True)
    return (o * lax.rsqrt(var + EPS) * nw * _silu(g)).astype(BF16)


def _ret_chunk(q_ref, k_ref, v_ref, g_ref, din_ref, qd_ref, kd_ref, cd_ref, s0_ref, nw_ref, o_ref, s_ref,
               is_first):
    @pl.when(is_first)
    def _():
        s_ref[...] = s0_ref[...]

    k_t = (k_ref[...] * kd_ref[...]).T
    for hh in range(2):
        qb = q_ref[:, hh * R_QK_DIM:(hh + 1) * R_QK_DIM].astype(BF16)
        kb = k_ref[:, hh * R_QK_DIM:(hh + 1) * R_QK_DIM].astype(BF16)
        vb = v_ref[:, hh * R_V_DIM:(hh + 1) * R_V_DIM].astype(BF16)
        s = s_ref[hh]
        inner = lax.dot_general(qb, kb, NT_DIMS, preferred_element_type=F32) * din_ref[hh]
        o = (jnp.dot(inner.astype(BF16), vb, preferred_element_type=F32)
             + jnp.dot(qb, s.astype(BF16), preferred_element_type=F32) * qd_ref[hh])
        kdt = k_t[hh * R_QK_DIM:(hh + 1) * R_QK_DIM, :].astype(BF16)
        s_ref[hh] = s * cd_ref[hh] + jnp.dot(kdt, vb, preferred_element_type=F32)
        o_ref[:, hh * R_V_DIM:(hh + 1) * R_V_DIM] = _ret_epilogue(
            o, g_ref[:, hh * R_V_DIM:(hh + 1) * R_V_DIM], nw_ref[...])


N_RET_INPUTS = 10


def _block_means(page_refs, o_ref):
    pages_per_block = len(page_refs) // o_ref.shape[0]
    for j in range(o_ref.shape[0]):
        acc = None
        rows = 0
        for u in range(pages_per_block):
            page = page_refs[j * pages_per_block + u]
            part = jnp.sum(page[...], axis=0)
            rows += page.shape[0]
            acc = part if acc is None else acc + part
        o_ref[j] = acc * (1.0 / rows)


def _moba_prompt_kernel(pt_ref, slope_ref, q_ref, k_ref, v_ref, *refs, seq, n_side_pages, layer, with_ret):
    refs = list(refs)
    pool_ref = refs.pop(0) if n_side_pages else None
    ret_in = [refs.pop(0) for _ in range(N_RET_INPUTS)] if with_ret else []
    o_ref = refs.pop(0)
    side_ref = refs.pop(0) if n_side_pages else None
    ret_out = [refs.pop(0) for _ in range(2)] if with_ret else []
    km_ref, vt_ref = refs.pop(0), refs.pop(0)
    if n_side_pages:
        pbuf, sem = refs
        step = (pl.program_id(0) * pl.num_programs(1) + pl.program_id(1)) * pl.num_programs(2) + pl.program_id(2)
        n_steps = pl.num_programs(0) * pl.num_programs(1) * pl.num_programs(2)
        steps_per_seq = pt_ref.shape[1] // n_side_pages

        def page_copies(step_i, slot_i):
            seq_i = step_i // steps_per_seq
            page0 = (step_i % steps_per_seq) * n_side_pages
            return [pltpu.make_async_copy(pool_ref.at[layer, pt_ref[seq_i, page0 + u]], pbuf.at[slot_i, u],
                                          sem.at[slot_i]) for u in range(n_side_pages)]

        slot = _ring_prefetch(step, n_steps, pbuf.shape[0], page_copies)
        _block_means([pbuf.at[slot, u] for u in range(n_side_pages)], side_ref)
    if with_ret:
        @pl.when(pl.program_id(1) % 2 == 0)
        def _():
            _ret_chunk(*ret_in, *ret_out, is_first=pl.program_id(2) == 0)

    qi = pl.program_id(2)
    blk = MOBA_BLOCK
    nb = seq // blk

    @pl.when(qi == 0)
    def _():
        km_ref[...] = jnp.mean(k_ref[...].reshape(nb, blk, A_HEAD_DIM), axis=1)
        vt_ref[...] = v_ref[...].T.astype(BF16)

    q_t = q_ref[...].T
    gate = jnp.dot(km_ref[...], q_t, precision=lax.Precision.HIGHEST,
                   preferred_element_type=F32)
    g = [jnp.where(n < qi, gate[n:n + 1, :], NEG) for n in range(nb)]
    scale = A_HEAD_DIM ** -0.5
    slope = slope_ref[...][:, :1]
    col_terms = []
    for n in range(nb):
        cnt = jnp.zeros((1, blk), F32)
        for m in range(nb):
            if m < n:
                cnt = cnt + jnp.where(g[m] >= g[n], 1.0, 0.0)
            elif m > n:
                cnt = cnt + jnp.where(g[m] > g[n], 1.0, 0.0)
        chosen = jnp.where(n < qi, jnp.where(cnt < MOBA_TOPK, 1.0, 0.0), jnp.where(n == qi, 1.0, 0.0))
        ahead = ((qi - n) * blk).astype(F32)
        col_terms.append(jnp.where(chosen > 0.5, -slope * ahead, NEG))

    rc = (lax.broadcasted_iota(I32, (blk, blk), 1) - lax.broadcasted_iota(I32, (blk, blk), 0))
    bias_past = -slope * rc.astype(F32)
    bias_own = jnp.where(rc >= 0, bias_past, NEG)
    q_tb = q_t.astype(BF16)

    def attend(n_keys_blocks):
        pieces = []
        for n in range(n_keys_blocks):
            kb = k_ref[n * blk:(n + 1) * blk, :].astype(BF16)
            s_t = jnp.dot(kb, q_tb, preferred_element_type=F32)
            pieces.append(s_t * scale + jnp.where(n == qi, bias_own, bias_past) + col_terms[n])
        logits = jnp.concatenate(pieces, axis=0)
        mx = jnp.max(logits, axis=0, keepdims=True)
        p = jnp.exp(logits - mx)
        denom = jnp.sum(p, axis=0, keepdims=True)
        out_t = jnp.dot(vt_ref[:, :n_keys_blocks * blk], p.astype(BF16), preferred_element_type=F32)
        o_ref[...] = (out_t / denom).T.astype(o_ref.dtype)

    half = nb // 2
    if half == 0:
        attend(nb)
    else:
        @pl.when(qi < half)
        def _():
            attend(half)

        @pl.when(qi >= half)
        def _():
            attend(nb)


def _side_pages_per_step(side_pool, side_page_table, n_steps):
    if side_pool is None:
        return 0
    bsz, n_pages = side_page_table.shape
    pages_per_block = MOBA_BLOCK // side_pool.shape[2]
    if (bsz * n_pages) % n_steps:
        return 0
    per_step = bsz * n_pages // n_steps
    if per_step % pages_per_block or n_pages % per_step or per_step > MAX_PAGES_PER_STEP:
        return 0
    return per_step


def _moba_prompt(qa, ka, va, slopes, batch, seq, side_pool=None, layer=0, side_page_table=None, ret=None):
    blk = MOBA_BLOCK
    nq = seq // blk
    n_side = _side_pages_per_step(side_pool, side_page_table, batch * A_HEADS * nq)
    in_specs = [pl.BlockSpec((None, 1, LANES), lambda b, h, i, pt: (h, 0, 0)),
                pl.BlockSpec((blk, A_HEAD_DIM), lambda b, h, i, pt: (b * nq + i, h)),
                pl.BlockSpec((seq, A_HEAD_DIM), lambda b, h, i, pt: (b, h)),
                pl.BlockSpec((seq, A_HEAD_DIM), lambda b, h, i, pt: (b, h))]
    out_specs = [pl.BlockSpec((blk, A_HEAD_DIM), lambda b, h, i, pt: (b * nq + i, h))]
    out_shape = [jax.ShapeDtypeStruct((batch * seq, A_WIDTH), BF16)]
    operands = [slopes, qa, ka, va]
    if n_side:
        page_size = side_pool.shape[2]
        side_bsz, n_pages = side_page_table.shape
        steps_per_seq = n_pages // n_side
        blocks_per_step = n_side // (MOBA_BLOCK // page_size)

        def flat(b, h, i):
            return (b * A_HEADS + h) * nq + i

        in_specs.append(pl.BlockSpec(memory_space=pl.ANY))
        operands.append(side_pool)
        out_specs.append(pl.BlockSpec(
            (None, blocks_per_step, A_HEADS, A_HEAD_DIM),
            lambda b, h, i, pt: (flat(b, h, i) // steps_per_seq, flat(b, h, i) % steps_per_seq, 0, 0)))
        out_shape.append(jax.ShapeDtypeStruct(
            (side_bsz, n_pages // (MOBA_BLOCK // page_size), A_HEADS, A_HEAD_DIM), F32))
        prefetch = side_page_table
        scratch = [pltpu.VMEM((RING_SLOTS, n_side, page_size, A_HEADS, A_HEAD_DIM), F32),
                   pltpu.SemaphoreType.DMA((RING_SLOTS,))]
        semantics = ("arbitrary", "arbitrary", "arbitrary")
    else:
        prefetch = jnp.zeros((1, 1), I32)
        scratch = []
        semantics = ("parallel", "parallel", "arbitrary")
    with_ret = bool(n_side) and ret is not None and A_HEADS == R_HEADS and seq // math.gcd(seq, RET_CHUNK) == nq
    if with_ret:
        def ret_index(b, h, i, pt):
            return (b, h // 2, jnp.where(h % 2 == 0, i, nq - 1))
        ret_operands, ret_in_specs, ret_out_specs, ret_out_shape, _ = _ret_chunk_operands(*ret, seq, ret_index)
        in_specs += ret_in_specs
        operands += ret_operands
        out_specs += ret_out_specs
        out_shape += ret_out_shape
    outs = pl.pallas_call(
        functools.partial(_moba_prompt_kernel, seq=seq, n_side_pages=n_side, layer=layer, with_ret=with_ret),
        grid_spec=pltpu.PrefetchScalarGridSpec(
            num_scalar_prefetch=1,
            grid=(batch, A_HEADS, nq),
            in_specs=in_specs,
            out_specs=out_specs,
            scratch_shapes=[pltpu.VMEM((nq, A_HEAD_DIM), F32),
                            pltpu.VMEM((A_HEAD_DIM, seq), BF16)]
            + scratch),
        out_shape=out_shape,
        compiler_params=_params(*semantics),
        name="moba_prompt",
    )(prefetch, *operands)
    return outs[0], (outs[1] if n_side else None), (tuple(outs[-2:]) if with_ret else None)


MAX_PAGES_PER_STEP = 16


def _page_mean_kernel(pt_ref, *refs):
    del pt_ref
    _block_means(refs[:-1], refs[-1])


def _page_means(pool, layer, page_table):
    page_size = pool.shape[2]
    bsz, n_pages = page_table.shape
    pages_per_block = MOBA_BLOCK // page_size
    pages_per_step = math.gcd(n_pages, MAX_PAGES_PER_STEP)
    assert pages_per_step % pages_per_block == 0
    blocks_per_step = pages_per_step // pages_per_block
    n_blk = n_pages // pages_per_block

    def page_spec(u):
        return pl.BlockSpec((None, None, page_size, A_HEADS, A_HEAD_DIM),
                            lambda b, g, pt: (layer, pt[b, g * pages_per_step + u], 0, 0, 0))

    return pl.pallas_call(
        _page_mean_kernel,
        grid_spec=pltpu.PrefetchScalarGridSpec(
            num_scalar_prefetch=1,
            grid=(bsz, n_pages // pages_per_step),
            in_specs=[page_spec(u) for u in range(pages_per_step)],
            out_specs=pl.BlockSpec((None, blocks_per_step, A_HEADS, A_HEAD_DIM),
                                   lambda b, g, pt: (b, g, 0, 0))),
        out_shape=jax.ShapeDtypeStruct((bsz, n_blk, A_HEADS, A_HEAD_DIM), F32),
        compiler_params=_params("parallel", "arbitrary"),
        name="page_means",
    )(page_table, *([pool] * pages_per_step))


def _block_choice_kernel(q_ref, km_ref, o_ref):
    rows = q_ref.shape[0]
    t_new = rows // A_HEADS
    n_blk = km_ref.shape[0]
    q = q_ref[...]
    r_head = lax.broadcasted_iota(I32, q.shape, 0) // t_new
    c_head = lax.broadcasted_iota(I32, q.shape, 1) // A_HEAD_DIM
    qbd = jnp.where(r_head == c_head, q, 0.0)
    gate = lax.dot_general(qbd, km_ref[...], NT_DIMS, precision=lax.Precision.HIGHEST,
                           preferred_element_type=F32)
    lane = lax.broadcasted_iota(I32, gate.shape, 1).astype(F32)
    out_lane = lax.broadcasted_iota(I32, (rows, LANES), 1)
    res = jnp.zeros((rows, LANES), F32)
    for it in range(MOBA_TOPK):
        mx = jnp.max(gate, axis=1, keepdims=True)
        idx = jnp.min(jnp.where(gate == mx, lane, float(n_blk)), axis=1, keepdims=True)
        res = jnp.where(out_lane == it, idx, res)
        gate = jnp.where(lane == idx, -jnp.inf, gate)
    o_ref[...] = res.astype(I32)


def _block_choice(q_rep, kmean):
    bsz, rows, _ = q_rep.shape
    n_blk = kmean.shape[1]
    return pl.pallas_call(
        _block_choice_kernel,
        grid=(bsz,),
        in_specs=[pl.BlockSpec((None, rows, A_WIDTH), lambda b: (b, 0, 0)),
                  pl.BlockSpec((None, n_blk, A_WIDTH), lambda b: (b, 0, 0))],
        out_specs=pl.BlockSpec((None, rows, LANES), lambda b: (b, 0, 0)),
        out_shape=jax.ShapeDtypeStruct((bsz, rows, LANES), I32),
        compiler_params=_params("parallel"),
        name="block_choice",
    )(q_rep, kmean)


def _moba_paged_kernel(pt_ref, blk_ref, slope_ref, q_ref, kn_ref, vn_ref, pk_ref, pv_ref, o_ref,
                       kbuf, vbuf, sem, *, layer, t_new, page_size, past_len):
    pages_per_block = MOBA_BLOCK // page_size
    tiles_per_q = MOBA_TOPK * pages_per_block
    n_heads = pl.num_programs(1)
    step = pl.program_id(0) * n_heads + pl.program_id(1)
    n_steps = pl.num_programs(0) * n_heads

    def chosen_block(step_i, t, j):
        return blk_ref[(step_i * t_new + t) * MOBA_TOPK + j]

    def tile_copies(step_i, slot_i):
        b = step_i // n_heads
        h = step_i % n_heads
        copies = []
        for t in range(t_new):
            for j in range(MOBA_TOPK):
                block = chosen_block(step_i, t, j)
                for half in range(pages_per_block):
                    page = pt_ref[b, block * pages_per_block + half]
                    idx = t * tiles_per_q + j * pages_per_block + half
                    copies.append(pltpu.make_async_copy(pk_ref.at[layer, page, :, h, :],
                                                        kbuf.at[slot_i, idx], sem.at[slot_i]))
                    copies.append(pltpu.make_async_copy(pv_ref.at[layer, page, :, h, :],
                                                        vbuf.at[slot_i, idx], sem.at[slot_i]))
        return copies

    slot = _ring_prefetch(step, n_steps, kbuf.shape[0], tile_copies)

    scale = A_HEAD_DIM ** -0.5
    slope = slope_ref[...][:, :1]
    row = lax.broadcasted_iota(I32, (page_size, 1), 0)
    row_bias = slope * row.astype(F32)
    row_new = lax.broadcasted_iota(I32, (t_new, 1), 0)
    kn = kn_ref[...]
    vn = vn_ref[...]
    for t in range(t_new):
        qt = q_ref[t:t + 1, :]
        cols = []
        for j in range(MOBA_TOPK):
            block = chosen_block(step, t, j)
            for half in range(pages_per_block):
                kt = kbuf[slot, t * tiles_per_q + j * pages_per_block + half]
                lg = jnp.sum(kt * qt, axis=-1, keepdims=True) * scale
                page_dist = (past_len + t - half * page_size) - block * MOBA_BLOCK
                cols.append((lg + row_bias) - slope * page_dist.astype(F32))
        dist_new = t - row_new
        lg_new = jnp.sum(kn * qt, axis=-1, keepdims=True) * scale - slope * dist_new.astype(F32)
        lg_new = jnp.where(dist_new >= 0, lg_new, NEG)
        mx = jnp.max(lg_new, axis=0, keepdims=True)
        for c in cols:
            mx = jnp.maximum(mx, jnp.max(c, axis=0, keepdims=True))
        p_new = jnp.exp(lg_new - mx)
        denom = jnp.sum(p_new, axis=0, keepdims=True)
        acc = jnp.sum(p_new * vn, axis=0, keepdims=True)
        for idx, c in enumerate(cols):
            p = jnp.exp(c - mx)
            denom = denom + jnp.sum(p, axis=0, keepdims=True)
            acc = acc + jnp.sum(p * vbuf[slot, t * tiles_per_q + idx], axis=0, keepdims=True)
        o_ref[t:t + 1, :] = acc / denom


def _moba_paged(qa, ka, va, pool_k, pool_v, layer, page_table, blocks, slopes):
    bsz, t_new, _ = qa.shape
    page_size = pool_k.shape[2]
    n_pages = page_table.shape[1]
    past_len = n_pages * page_size
    n_tiles = t_new * MOBA_TOPK * (MOBA_BLOCK // page_size)
    new_spec = pl.BlockSpec((None, t_new, A_HEAD_DIM), lambda b, h, pt, blk: (b, 0, h))
    hbm_spec = pl.BlockSpec(memory_space=pl.ANY)
    return pl.pallas_call(
        functools.partial(_moba_paged_kernel, layer=layer, t_new=t_new, page_size=page_size,
                          past_len=past_len),
        grid_spec=pltpu.PrefetchScalarGridSpec(
            num_scalar_prefetch=2,
            grid=(bsz, A_HEADS),
            in_specs=[pl.BlockSpec((None, 1, LANES), lambda b, h, pt, blk: (h, 0, 0)),
                      new_spec, new_spec, new_spec, hbm_spec, hbm_spec],
            out_specs=new_spec,
            scratch_shapes=[pltpu.VMEM((RING_SLOTS, n_tiles, page_size, A_HEAD_DIM), F32),
                            pltpu.VMEM((RING_SLOTS, n_tiles, page_size, A_HEAD_DIM), F32),
                            pltpu.SemaphoreType.DMA((RING_SLOTS,))]),
        out_shape=jax.ShapeDtypeStruct((bsz, t_new, A_WIDTH), F32),
        compiler_params=_params("arbitrary", "arbitrary"),
        name="moba_paged",
    )(page_table, blocks, slopes, qa, ka, va, pool_k, pool_v)


def _ret_chunk_kernel(*refs):
    _ret_chunk(*refs, is_first=pl.program_id(2) == 0)


def _ret_tables(chunk):
    log_gamma = jnp.log1p(-jnp.exp2(-5.0 - jnp.arange(R_HEADS, dtype=F32)))
    idx = jnp.arange(chunk, dtype=F32)
    diff = idx[:, None] - idx[None, :]
    causal = diff >= 0
    decay_in = jnp.where(causal[None], jnp.exp(jnp.where(causal, diff, 0.0)[None] * log_gamma[:, None, None]), 0.0)
    q_dec = jnp.exp((idx + 1)[None, :] * log_gamma[:, None])
    k_dec = jnp.exp((chunk - 1 - idx)[None, :] * log_gamma[:, None])
    chunk_dec = jnp.exp(chunk * log_gamma)
    return decay_in, q_dec, k_dec, chunk_dec


def _pair_lanes(k_dec):
    h, c = k_dec.shape
    t = jnp.broadcast_to(k_dec.reshape(h // 2, 2, c, 1), (h // 2, 2, c, R_QK_DIM))
    return t.transpose(0, 2, 1, 3).reshape(h // 2, c, 2 * R_QK_DIM)


def _ret_chunk_operands(r, s0, ret_norm_w, seq, index):
    batch = s0.shape[0]
    c = math.gcd(seq, RET_CHUNK)
    nc = seq // c
    decay_in, q_dec, k_dec, chunk_dec = _ret_tables(c)
    qd = q_dec[:, :, None]
    kd = _pair_lanes(k_dec)
    cd = jnp.broadcast_to(chunk_dec[:, None, None], (R_HEADS, 1, LANES))
    qk_blocks = R_QK_WIDTH // LANES
    v_off = 2 * R_QK_WIDTH // (2 * R_V_DIM)
    g_off = (2 * R_QK_WIDTH + R_WIDTH) // (2 * R_V_DIM)

    def rows(col0):
        def imap(*ids):
            b, p, i = index(*ids)
            return (b * nc + i, col0 + p)
        return imap

    def pair(*tail):
        def imap(*ids):
            return (index(*ids)[1],) + tail
        return imap

    def state(*ids):
        b, p, _ = index(*ids)
        return (b, p, 0, 0)

    in_specs = [pl.BlockSpec((c, LANES), rows(0)),
                pl.BlockSpec((c, LANES), rows(qk_blocks)),
                pl.BlockSpec((c, 2 * R_V_DIM), rows(v_off)),
                pl.BlockSpec((c, 2 * R_V_DIM), rows(g_off)),
                pl.BlockSpec((2, c, c), pair(0, 0)),
                pl.BlockSpec((2, c, 1), pair(0, 0)),
                pl.BlockSpec((None, c, LANES), pair(0, 0)),
                pl.BlockSpec((2, 1, LANES), pair(0, 0)),
                pl.BlockSpec((None, 2, R_QK_DIM, R_V_DIM), state),
                pl.BlockSpec((1, R_V_DIM), lambda *ids: (0, 0))]
    out_specs = [pl.BlockSpec((c, 2 * R_V_DIM), rows(0)),
                 pl.BlockSpec((None, 2, R_QK_DIM, R_V_DIM), state)]
    out_shape = [jax.ShapeDtypeStruct((batch * seq, R_WIDTH), BF16),
                 jax.ShapeDtypeStruct((batch, R_HEADS, R_QK_DIM, R_V_DIM), F32)]
    operands = [r, r, r, r, decay_in, qd, kd, cd, s0, ret_norm_w]
    assert len(operands) == N_RET_INPUTS
    return operands, in_specs, out_specs, out_shape, nc


def _retention_prompt(r, s0, ret_norm_w, batch, seq):
    operands, in_specs, out_specs, out_shape, nc = _ret_chunk_operands(
        r, s0, ret_norm_w, seq, lambda b, p, i: (b, p, i))
    return pl.pallas_call(
        _ret_chunk_kernel,
        grid=(batch, R_HEADS // 2, nc),
        in_specs=in_specs,
        out_specs=out_specs,
        out_shape=out_shape,
        compiler_params=_params("parallel", "parallel", "arbitrary"),
        name="retention_chunks",
    )(*operands)


def _ret_step_kernel(q_ref, k_ref, v_ref, g_ref, din_ref, qd_ref, kd_ref, cd_ref, s0_ref, nw_ref,
                     o_ref, s_ref, *, t_new):
    rows = q_ref.shape[0]
    bsz = rows // t_new
    k_t = (k_ref[...] * kd_ref[...]).T
    seq_of_row = lax.broadcasted_iota(I32, (rows, bsz * R_QK_DIM), 0) // t_new
    seq_of_col = lax.broadcasted_iota(I32, (rows, bsz * R_QK_DIM), 1) // R_QK_DIM
    own_q = seq_of_row == seq_of_col
    seq_of_srow = lax.broadcasted_iota(I32, (bsz * R_QK_DIM, rows), 0) // R_QK_DIM
    seq_of_scol = lax.broadcasted_iota(I32, (bsz * R_QK_DIM, rows), 1) // t_new
    own_k = seq_of_srow == seq_of_scol
    for hh in range(2):
        q = q_ref[:, hh * R_QK_DIM:(hh + 1) * R_QK_DIM]
        qb = q.astype(BF16)
        kb = k_ref[:, hh * R_QK_DIM:(hh + 1) * R_QK_DIM].astype(BF16)
        vb = v_ref[:, hh * R_V_DIM:(hh + 1) * R_V_DIM].astype(BF16)
        s = s0_ref[:, hh].reshape(bsz * R_QK_DIM, R_V_DIM)
        inner = lax.dot_general(qb, kb, NT_DIMS, preferred_element_type=F32) * din_ref[hh]
        q_wide = jnp.where(own_q, jnp.concatenate([q] * bsz, axis=1), 0.0).astype(BF16)
        o = (jnp.dot(inner.astype(BF16), vb, preferred_element_type=F32)
             + jnp.dot(q_wide, s.astype(BF16), preferred_element_type=F32) * qd_ref[hh])
        kdt = k_t[hh * R_QK_DIM:(hh + 1) * R_QK_DIM, :]
        k_tall = jnp.where(own_k, jnp.concatenate([kdt] * bsz, axis=0), 0.0).astype(BF16)
        s_new = s * cd_ref[hh] + jnp.dot(k_tall, vb, preferred_element_type=F32)
        s_ref[:, hh] = s_new.reshape(bsz, R_QK_DIM, R_V_DIM)
        o_ref[:, hh * R_V_DIM:(hh + 1) * R_V_DIM] = _ret_epilogue(
            o, g_ref[:, hh * R_V_DIM:(hh + 1) * R_V_DIM], nw_ref[...])


def _retention_step(r, s0, ret_norm_w, batch, t_new):
    rows = batch * t_new
    decay_in, q_dec, k_dec, chunk_dec = _ret_tables(t_new)
    same_seq = jnp.eye(batch, dtype=F32)
    din = jnp.einsum("hij,ab->haibj", decay_in, same_seq).reshape(R_HEADS, rows, rows)
    qd = jnp.tile(q_dec, (1, batch))[:, :, None]
    kd = _pair_lanes(jnp.tile(k_dec, (1, batch)))
    cd = jnp.broadcast_to(chunk_dec[:, None, None], (R_HEADS, 1, LANES))
    qk_blocks = R_QK_WIDTH // LANES
    v_off = 2 * R_QK_WIDTH // (2 * R_V_DIM)
    g_off = (2 * R_QK_WIDTH + R_WIDTH) // (2 * R_V_DIM)
    return pl.pallas_call(
        functools.partial(_ret_step_kernel, t_new=t_new),
        grid=(R_HEADS // 2,),
        in_specs=[pl.BlockSpec((rows, LANES), lambda p: (0, p)),
                  pl.BlockSpec((rows, LANES), lambda p: (0, qk_blocks + p)),
                  pl.BlockSpec((rows, 2 * R_V_DIM), lambda p: (0, v_off + p)),
                  pl.BlockSpec((rows, 2 * R_V_DIM), lambda p: (0, g_off + p)),
                  pl.BlockSpec((2, rows, rows), lambda p: (p, 0, 0)),
                  pl.BlockSpec((2, rows, 1), lambda p: (p, 0, 0)),
                  pl.BlockSpec((None, rows, LANES), lambda p: (p, 0, 0)),
                  pl.BlockSpec((2, 1, LANES), lambda p: (p, 0, 0)),
                  pl.BlockSpec((batch, 2, R_QK_DIM, R_V_DIM), lambda p: (0, p, 0, 0)),
                  pl.BlockSpec((1, R_V_DIM), lambda p: (0, 0))],
        out_specs=[pl.BlockSpec((rows, 2 * R_V_DIM), lambda p: (0, p)),
                   pl.BlockSpec((batch, 2, R_QK_DIM, R_V_DIM), lambda p: (0, p, 0, 0))],
        out_shape=[jax.ShapeDtypeStruct((rows, R_WIDTH), BF16),
                   jax.ShapeDtypeStruct((batch, R_HEADS, R_QK_DIM, R_V_DIM), F32)],
        compiler_params=_params("parallel"),
        name="retention_step",
    )(r, r, r, r, din, qd, kd, cd, s0, ret_norm_w)


def _outproj_kernel(a_ref, r_ref, x_ref, g1_ref, sh_ref, sc_ref, nw_ref, wa_ref, wr_ref,
                    x1_ref, h2_ref):
    mixed = (jnp.dot(a_ref[...].astype(BF16), wa_ref[...], preferred_element_type=F32)
             + jnp.dot(r_ref[...].astype(BF16), wr_ref[...], preferred_element_type=F32))
    x1 = x_ref[...] + g1_ref[...] * mixed
    x1_ref[...] = x1
    var = jnp.mean(x1 * x1, axis=-1, keepdims=True)
    h2 = x1 * lax.rsqrt(var + EPS) * nw_ref[...]
    h2_ref[...] = (h2 * (1 + sc_ref[...]) + sh_ref[...]).astype(BF16)


def _outproj(out_a, out_r, x, gate, shift, scale, norm_w, w_out, tm, tiles_per_mod):
    m, d = x.shape
    mod_rows = shift.shape[1]
    mod_spec = pl.BlockSpec((None, mod_rows, d), lambda i: (i // tiles_per_mod, 0, 0))
    return pl.pallas_call(
        _outproj_kernel,
        grid=(m // tm,),
        in_specs=[pl.BlockSpec((tm, A_WIDTH), lambda i: (i, 0)),
                  pl.BlockSpec((tm, R_WIDTH), lambda i: (i, 0)),
                  pl.BlockSpec((tm, d), lambda i: (i, 0)),
                  mod_spec, mod_spec, mod_spec,
                  pl.BlockSpec((1, d), lambda i: (0, 0)),
                  pl.BlockSpec((A_WIDTH, d), lambda i: (0, 0)),
                  pl.BlockSpec((R_WIDTH, d), lambda i: (A_WIDTH // R_WIDTH, 0))],
        out_specs=[pl.BlockSpec((tm, d), lambda i: (i, 0)),
                   pl.BlockSpec((tm, d), lambda i: (i, 0))],
        out_shape=[jax.ShapeDtypeStruct((m, d), F32), jax.ShapeDtypeStruct((m, d), BF16)],
        compiler_params=_params("parallel"),
        name="outproj",
    )(out_a, out_r, x, gate, shift, scale, norm_w, w_out, w_out)


def _pruned_pairs():
    return [(a, P_TOPK // (a + 1)) for a in range(P_TOPK)]


N_CAND = sum(nb for _, nb in _pruned_pairs())
CAND_ROWS = -(-N_CAND // 8) * 8


def _top_rows(x, k, val_ref, idx_ref):
    nrows = x.shape[0]
    row = lax.broadcasted_iota(I32, x.shape, 0).astype(F32)
    for i in range(k):
        mx = jnp.max(x, axis=0, keepdims=True)
        am = jnp.min(jnp.where(x == mx, row, float(nrows)), axis=0, keepdims=True)
        val_ref[i:i + 1, :] = mx
        idx_ref[i:i + 1, :] = am
        x = jnp.where(row == am, -jnp.inf, x)


def _peer_route_kernel(h_ref, wq_ref, sk_ref, i1_ref, i2_ref, g_ref,
                       s1_ref, k1_ref, s2_ref, k2_ref, cand_ref, c1_ref, c2_ref):
    tm = h_ref.shape[0]
    qp = jnp.dot(h_ref[...], wq_ref[...], preferred_element_type=F32)
    cand_ref[...] = jnp.full(cand_ref.shape, -jnp.inf, F32)
    c1_ref[...] = jnp.zeros(c1_ref.shape, F32)
    c2_ref[...] = jnp.zeros(c2_ref.shape, F32)
    row = lax.broadcasted_iota(I32, (CAND_ROWS, tm), 0).astype(F32)
    for hd in range(P_HEADS):
        for half, (s_ref, k_ref) in enumerate(((s1_ref, k1_ref), (s2_ref, k2_ref))):
            col = (hd * 2 + half) * LANES
            q_part = qp[:, col:col + LANES].astype(BF16)
            scores = lax.dot_general(sk_ref[hd * 2 + half], q_part, NT_DIMS,
                                     preferred_element_type=F32)
            _top_rows(scores, P_TOPK, s_ref, k_ref)
        off = 0
        for a, nb in _pruned_pairs():
            cand_ref[off:off + nb, :] = s1_ref[a:a + 1, :] + s2_ref[0:nb, :]
            c1_ref[off:off + nb, :] = jnp.broadcast_to(k1_ref[a:a + 1, :], (nb, tm))
            c2_ref[off:off + nb, :] = k2_ref[0:nb, :]
            off += nb
        cand = cand_ref[...]
        c1 = c1_ref[...]
        c2 = c2_ref[...]
        best, e1, e2 = [], [], []
        for _ in range(P_TOPK):
            mx = jnp.max(cand, axis=0, keepdims=True)
            am = jnp.min(jnp.where(cand == mx, row, float(CAND_ROWS)), axis=0, keepdims=True)
            hit = row == am
            best.append(mx)
            e1.append(jnp.max(jnp.where(hit, c1, -1.0), axis=0, keepdims=True))
            e2.append(jnp.max(jnp.where(hit, c2, -1.0), axis=0, keepdims=True))
            cand = jnp.where(hit, -jnp.inf, cand)
        top = best[0]
        exps = [jnp.exp(v - top) for v in best]
        denom = exps[0]
        for v in exps[1:]:
            denom = denom + v
        for i in range(P_TOPK):
            slot = hd * P_TOPK + i
            i1_ref[slot:slot + 1, :] = e1[i].astype(I32)
            i2_ref[slot:slot + 1, :] = e2[i].astype(I32)
            g_ref[slot:slot + 1, :] = exps[i] / denom


def _peer_route(h2, w_query, sub_keys, tm):
    m, d = h2.shape
    qd = w_query.shape[1]
    slot_spec = pl.BlockSpec((P_SLOTS, tm), lambda i: (0, i))
    return pl.pallas_call(
        _peer_route_kernel,
        grid=(m // tm,),
        in_specs=[pl.BlockSpec((tm, d), lambda i: (i, 0)),
                  pl.BlockSpec((d, qd), lambda i: (0, 0)),
                  pl.BlockSpec(sub_keys.shape, lambda i: (0, 0, 0))],
        out_specs=[slot_spec, slot_spec, slot_spec],
        out_shape=[jax.ShapeDtypeStruct((P_SLOTS, m), I32),
                   jax.ShapeDtypeStruct((P_SLOTS, m), I32),
                   jax.ShapeDtypeStruct((P_SLOTS, m), F32)],
        scratch_shapes=[pltpu.VMEM((P_TOPK, tm), F32)] * 4 + [pltpu.VMEM((CAND_ROWS, tm), F32)] * 3,
        compiler_params=_params("parallel"),
        name="peer_route",
    )(h2, w_query, sub_keys)


SUBLANES = 8


def _peer_mask_kernel(i1_ref, i2_ref, g_ref, o_ref):
    key = lax.broadcasted_iota(I32, (P_NKEYS, P_SLOTS), 0)

    def body(grp, carry):
        base = pl.multiple_of(grp * SUBLANES, SUBLANES)
        mats = []
        for u in range(SUBLANES):
            i1 = i1_ref[pl.ds(base + u, 1), :]
            i2 = i2_ref[pl.ds(base + u, 1), :]
            g = g_ref[pl.ds(base + u, 1), :]
            g_hi = g.astype(BF16).astype(F32)
            g_lo = g - g_hi
            hit1 = key == i1
            a = jnp.concatenate([jnp.where(hit1, g_hi, 0.0).astype(BF16),
                                 jnp.where(hit1, g_lo, 0.0).astype(BF16)], axis=1)
            onehot2 = jnp.where(key == i2, 1.0, 0.0).astype(BF16)
            bm = jnp.concatenate([onehot2, onehot2], axis=1)
            mats.append(lax.dot_general(a, bm, NT_DIMS, preferred_element_type=F32))
        o_ref[grp] = jnp.swapaxes(jnp.stack(mats, axis=0), 0, 1)
        return carry

    lax.fori_loop(0, o_ref.shape[0], body, 0, unroll=8)


def _peer_mask(i1, i2, gates, tt):
    m = i1.shape[0]
    slot_spec = pl.BlockSpec((tt, P_SLOTS), lambda i: (i, 0))
    return pl.pallas_call(
        _peer_mask_kernel,
        grid=(m // tt,),
        in_specs=[slot_spec, slot_spec, slot_spec],
        out_specs=pl.BlockSpec((tt // SUBLANES, P_NKEYS, SUBLANES, P_NKEYS), lambda i: (i, 0, 0, 0)),
        out_shape=jax.ShapeDtypeStruct((m // SUBLANES, P_NKEYS, SUBLANES, P_NKEYS), F32),
        compiler_params=_params("parallel"),
        name="peer_mask",
    )(i1, i2, gates)


def _peer_mlp_kernel(h_ref, dn_ref, up_ref, m_ref, x_ref, g2_ref, o_ref, acc_ref):
    e = pl.program_id(1)

    @pl.when(e == 0)
    def _():
        acc_ref[...] = jnp.zeros(acc_ref.shape, F32)

    pre = lax.dot_general(h_ref[...], dn_ref[...], NT_DIMS, preferred_element_type=F32)
    act = 0.5 * pre * (1.0 + lax.erf(pre * (0.5 ** 0.5)))
    tm = act.shape[0]
    parts = []
    for a in range(m_ref.shape[1]):
        gate = m_ref[:, a].reshape(tm, P_NKEYS)
        parts.append((act[:, a * P_NKEYS:(a + 1) * P_NKEYS] * gate).astype(BF16))
    w = jnp.concatenate(parts, axis=1)
    acc_ref[...] += jnp.dot(w, up_ref[...], preferred_element_type=F32)

    @pl.when(e == pl.num_programs(1) - 1)
    def _():
        o_ref[...] = x_ref[...] + g2_ref[...] * acc_ref[...]


def _peer_mlp(h2, down, up, mask, x1, gate, tm, te, tiles_per_mod):
    m, d = h2.shape
    n_exp = down.shape[0]
    mod_rows = gate.shape[1]
    assert te % P_NKEYS == 0 and tm % SUBLANES == 0
    return pl.pallas_call(
        _peer_mlp_kernel,
        grid=(m // tm, n_exp // te),
        in_specs=[pl.BlockSpec((tm, d), lambda i, e: (i, 0)),
                  pl.BlockSpec((te, d), lambda i, e: (e, 0)),
                  pl.BlockSpec((te, d), lambda i, e: (e, 0)),
                  pl.BlockSpec((tm // SUBLANES, te // P_NKEYS, SUBLANES, P_NKEYS),
                               lambda i, e: (i, e, 0, 0)),
                  pl.BlockSpec((tm, d), lambda i, e: (i, 0), pipeline_mode=pl.Buffered(1)),
                  pl.BlockSpec((None, mod_rows, d), lambda i, e: (i // tiles_per_mod, 0, 0))],
        out_specs=pl.BlockSpec((tm, d), lambda i, e: (i, 0)),
        out_shape=jax.ShapeDtypeStruct((m, d), F32),
        scratch_shapes=[pltpu.VMEM((tm, d), F32)],
        compiler_params=_params("parallel", "arbitrary"),
        name="peer_mlp",
    )(h2, down, up, mask, x1, gate)


def _alibi_slope_rows():
    slopes = jnp.exp2(-8.0 * jnp.arange(1, A_HEADS + 1, dtype=F32) / A_HEADS)
    return jnp.broadcast_to(slopes[:, None, None], (A_HEADS, 1, LANES))


def _layer(x, mod, pool_k, pool_v, layer, page_table, s0, w, *, tm, tm_mlp, side=None, kmean=None):
    bsz, seq, d = x.shape
    m = bsz * seq
    xf = x.reshape(m, d)
    chunks = jnp.split(mod, 6, axis=-1)
    if seq % tm == 0:
        mods = [c[:, None, :] for c in chunks]
        tiles_per_mod = seq // tm
        tiles_per_mod_mlp = seq // tm_mlp
    else:
        assert m == tm == tm_mlp
        mods = [jnp.repeat(c, seq, axis=0)[None] for c in chunks]
        tiles_per_mod = tiles_per_mod_mlp = 1
    sh1, sc1, g1, sh2, sc2, g2 = mods

    proj = functools.partial(_inproj, xf, w["norm1_w"], sh1, sc1, w["w_in"], tm=tm,
                             tiles_per_mod=tiles_per_mod)
    qa = proj(col0=0, ncols=A_WIDTH, aux=w["q_norm_w"], mode="headnorm")
    ka = proj(col0=A_WIDTH, ncols=A_WIDTH, aux=w["k_norm_w"], mode="headnorm")
    va = proj(col0=2 * A_WIDTH, ncols=A_WIDTH, aux=w["ones_a"], mode="scale")
    r = proj(col0=3 * A_WIDTH, ncols=2 * R_QK_WIDTH + 2 * R_WIDTH, aux=w["r_scale"], mode="scale")

    slopes = _alibi_slope_rows()
    if pool_k is None:
        side_pool, side_table = side if side is not None else (None, None)
        out_a, side_kmean, fused_ret = _moba_prompt(qa, ka, va, slopes, bsz, seq, side_pool, layer, side_table,
                                                    ret=(r, s0, w["ret_norm_w"]))
        out_r, s_new = fused_ret if fused_ret is not None else _retention_prompt(r, s0, w["ret_norm_w"], bsz, seq)
    else:
        side_kmean = None
        if kmean is None:
            kmean = _page_means(pool_k, layer, page_table)
        q3 = qa.reshape(bsz, seq, A_WIDTH)
        q_rep = jnp.broadcast_to(q3[:, None], (bsz, A_HEADS, seq, A_WIDTH)).reshape(bsz, A_HEADS * seq, A_WIDTH)
        choice = _block_choice(q_rep, kmean.reshape(bsz, kmean.shape[1], A_WIDTH))
        blocks = choice[:, :, :MOBA_TOPK].reshape(-1)
        out_a = _moba_paged(q3, ka.reshape(bsz, seq, A_WIDTH), va.reshape(bsz, seq, A_WIDTH),
                            pool_k, pool_v, layer, page_table, blocks, slopes).reshape(m, A_WIDTH)
        out_r, s_new = _retention_step(r, s0, w["ret_norm_w"], bsz, seq)

    tm_small = min(tm, 256)
    x1, h2 = _outproj(out_a, out_r, xf, g1, sh2, sc2, w["norm2_w"], w["w_out"], tm_small,
                      tiles_per_mod * (tm // tm_small))

    i1, i2, gates = _peer_route(h2, w["peer_w_query"], w["peer_sub_keys"], tm=tm_small)
    mask = _peer_mask(i1.T, i2.T, gates.T, tt=min(m, 64))
    y = _peer_mlp(h2, w["peer_down"], w["peer_up"], mask, x1, g2,
                  tm=tm_mlp, te=1024, tiles_per_mod=tiles_per_mod_mlp)
    k_out = ka.reshape(bsz, seq, A_HEADS, A_HEAD_DIM)
    v_out = va.reshape(bsz, seq, A_HEADS, A_HEAD_DIM)
    return y.reshape(bsz, seq, d), k_out, v_out, s_new, side_kmean


def kernel(x_prompt, x_sample, cache_k, cache_v, state_ret, page_table, c_prompt, c_sample, w_ada, b_ada, norm1_w, w_in, q_norm_w, k_norm_w, ret_norm_w, w_out, norm2_w, peer_w_query, peer_sub_keys, peer_down, peer_up):
    depth = w_ada.shape[0]
    n_prompt = c_prompt.shape[0]
    n_sample = c_sample.shape[0]
    pad = (-(n_prompt + n_sample)) % 8
    d = x_prompt.shape[-1]
    hp, hs = x_prompt, x_sample
    s0_prompt = jnp.zeros((n_prompt, R_HEADS, R_QK_DIM, R_V_DIM), F32)
    r_scale = jnp.concatenate([jnp.ones((1, R_QK_WIDTH), F32),
                               jnp.full((1, R_QK_WIDTH), R_QK_DIM ** -0.5, F32),
                               jnp.ones((1, 2 * R_WIDTH), F32)], axis=1)
    outs = [[] for _ in range(6)]
    for l in range(depth):
        c_all = jnp.concatenate([c_prompt, c_sample, jnp.zeros((pad, d), F32)], axis=0)
        mod = _ada(c_all, w_ada[l], b_ada[l][None])
        w = dict(norm1_w=norm1_w[l][None], w_in=w_in[l].astype(BF16), q_norm_w=q_norm_w[l][None],
                 k_norm_w=k_norm_w[l][None], ret_norm_w=ret_norm_w[l][None],
                 w_out=w_out[l].astype(BF16), norm2_w=norm2_w[l][None],
                 peer_w_query=peer_w_query[l].astype(BF16),
                 peer_sub_keys=peer_sub_keys[l].reshape(2 * P_HEADS, P_NKEYS, -1).astype(BF16),
                 peer_down=peer_down[l].astype(BF16), peer_up=peer_up[l].astype(BF16),
                 ones_a=jnp.ones((1, A_WIDTH), F32), r_scale=r_scale)
        hp, k1, v1, s1, kmean = _layer(hp, mod[:n_prompt], None, None, l, None, s0_prompt, w, tm=512,
                                       tm_mlp=512, side=(cache_k, page_table))
        n_new = hs.shape[0] * hs.shape[1]
        hs, k2, v2, s2, _ = _layer(hs, mod[n_prompt:n_prompt + n_sample], cache_k, cache_v, l, page_table,
                                   state_ret[l], w, kmean=kmean, tm=n_new, tm_mlp=n_new)
        for lst, val in zip(outs, (k1, v1, s1, k2, v2, s2)):
            lst.append(val)
    return (hp, hs) + tuple(jnp.stack(o) for o in outs)
```
